```python
import jax, jax.numpy as jnp
from jax import lax
import numpy as np


D_MODEL = 1024
BATCH = 16
SEQ = 256
DEPTH = 4
DEC_BATCH = 8
DEC_SEQ = 1024
PAST_LEN = 256

GRID_W = 64
N_HEADS = 8
N_KV_HEADS = 2
HEAD_DIM = 64
GQA_GROUP = N_HEADS // N_KV_HEADS
ATTN_W = N_HEADS * HEAD_DIM
KV_W = N_KV_HEADS * HEAD_DIM
POOL_WINDOWS = (2, 4, 8, 16)
N_POOL_GROUPS = 4
POOL_W = D_MODEL // 2
POOL_GROUP_W = POOL_W // N_POOL_GROUPS
MIX_W = ATTN_W + POOL_W
IN_W = ATTN_W + 2 * KV_W + POOL_W
D_FF = 2816
CONV_W = 3
WINDOW = 128
BLOCK = 128
ROPE_BASE = 10000.0
LN_EPS = 1e-5
DEEPNORM_ALPHA = (2 * DEPTH) ** 0.25
DEEPNORM_BETA = (8 * DEPTH) ** -0.25
ATTN_SCALE = HEAD_DIM ** -0.5
NEG_INF = -1e30

kernel_name = 'hybrid_pool_swa_prefix_dit_step'


def _layer_norm(x, g, b):
    xf = x.astype(jnp.float32)
    mu = jnp.mean(xf, axis=-1, keepdims=True)
    var = jnp.mean(jnp.square(xf - mu), axis=-1, keepdims=True)
    y = (xf - mu) * lax.rsqrt(var + LN_EPS) * g.astype(jnp.float32) + b.astype(jnp.float32)
    return y.astype(x.dtype)


def _modulation(cond, w_mod, b_mod):
    return jnp.split(jax.nn.silu(cond) @ w_mod + b_mod, 6, axis=-1)


def _project(h, w_in):
    B, T, _ = h.shape
    q, k, v, p = jnp.split(h @ w_in, [ATTN_W, ATTN_W + KV_W, ATTN_W + 2 * KV_W], axis=-1)
    q = q.reshape(B, T, N_HEADS, HEAD_DIM)
    k = k.reshape(B, T, N_KV_HEADS, HEAD_DIM)
    v = v.reshape(B, T, N_KV_HEADS, HEAD_DIM)
    return q, k, v, p


def _axial_rope(x):
    T = x.shape[1]
    rows = T // GRID_W
    row = jnp.repeat(jnp.arange(rows), GRID_W)
    col = jnp.tile(jnp.arange(GRID_W), rows)
    half = HEAD_DIM // 2
    inv_freq = ROPE_BASE ** (-jnp.arange(0, half, 2, dtype=jnp.float32) / half)

    def rotate(xa, pos):
        ang = pos.astype(jnp.float32)[:, None] * inv_freq[None, :]
        ang = jnp.concatenate([ang, ang], axis=-1)[None, :, None, :]
        cos = jnp.cos(ang).astype(x.dtype)
        sin = jnp.sin(ang).astype(x.dtype)
        x1, x2 = jnp.split(xa, 2, axis=-1)
        return xa * cos + jnp.concatenate([-x2, x1], axis=-1) * sin

    return jnp.concatenate([rotate(x[..., :half], row), rotate(x[..., half:], col)], axis=-1)


def _attend(q, k, v, sink, bias):
    s = jnp.einsum('bqhgd,bkhd->bhgqk', q, k).astype(jnp.float32) * ATTN_SCALE
    if bias is not None:
        s = s + bias
    sink_col = jnp.broadcast_to(sink.astype(jnp.float32)[None, :, :, None, None], s.shape[:-1] + (1,))
    p = jax.nn.softmax(jnp.concatenate([sink_col, s], axis=-1), axis=-1)[..., 1:]
    return jnp.einsum('bhgqk,bkhd->bqhgd', p.astype(v.dtype), v)


def _context_attention(q, k, v, sink):
    B, T = q.shape[:2]
    nb = T // BLOCK
    qb = jnp.moveaxis(q.reshape(B, nb, BLOCK, N_KV_HEADS, GQA_GROUP, HEAD_DIM), 1, 0)
    sk = sink.reshape(N_KV_HEADS, GQA_GROUP)
    out = lax.map(lambda qi: _attend(qi, k, v, sk, None), qb)
    return jnp.moveaxis(out, 0, 1).reshape(B, T, ATTN_W)


def _latent_attention(q, k, v, k_ctx, v_ctx, sink):
    B, T = q.shape[:2]
    nb = T // BLOCK
    Lc = k_ctx.shape[1]
    qg = q.reshape(B, T, N_KV_HEADS, GQA_GROUP, HEAD_DIM)
    pad = ((0, 0), (BLOCK, BLOCK), (0, 0), (0, 0))
    kp = jnp.pad(k, pad)
    vp = jnp.pad(v, pad)
    sk = sink.reshape(N_KV_HEADS, GQA_GROUP)
    r = jnp.arange(BLOCK)[:, None]
    cidx = jnp.arange(3 * BLOCK)[None, :]
    ctx_bias = jnp.zeros((BLOCK, Lc), jnp.float32)

    def block(i):
        qi = lax.dynamic_slice_in_dim(qg, i * BLOCK, BLOCK, axis=1)
        ki = lax.dynamic_slice_in_dim(kp, i * BLOCK, 3 * BLOCK, axis=1)
        vi = lax.dynamic_slice_in_dim(vp, i * BLOCK, 3 * BLOCK, axis=1)
        qpos = i * BLOCK + r
        kpos = (i - 1) * BLOCK + cidx
        valid = (jnp.abs(qpos - kpos) <= WINDOW) & (kpos >= 0) & (kpos < T)
        bias = jnp.concatenate([ctx_bias, jnp.where(valid, 0.0, NEG_INF).astype(jnp.float32)], axis=1)
        kk = jnp.concatenate([k_ctx.astype(ki.dtype), ki], axis=1)
        vv = jnp.concatenate([v_ctx.astype(vi.dtype), vi], axis=1)
        return _attend(qi, kk, vv, sk, bias)

    out = lax.map(block, jnp.arange(nb))
    return jnp.moveaxis(out, 0, 1).reshape(B, T, ATTN_W)


def _pool_mixer(p, w_pool, pool_scale):
    B, T, _ = p.shape
    pf = p.astype(jnp.float32).reshape(B, T, N_POOL_GROUPS, POOL_GROUP_W)
    cs = jnp.pad(jnp.cumsum(pf, axis=1), ((0, 0), (1, 0), (0, 0), (0, 0)))
    t = jnp.arange(T)[:, None]
    win = jnp.array(POOL_WINDOWS, dtype=jnp.int32)[None, :]
    start = jnp.maximum(t - win // 2, 0)
    end = jnp.minimum(t + win - win // 2, T)
    grp = jnp.arange(N_POOL_GROUPS)[None, :]
    total = cs[:, end, grp] - cs[:, start, grp]
    mean = total / (end - start).astype(jnp.float32)[None, :, :, None]
    d = (mean - pf).astype(p.dtype)
    y = jnp.einsum('btgc,gcd->btgd', d, w_pool).reshape(B, T, POOL_W)
    return y * pool_scale


def _conv_ffn(h, w_up, conv_w, conv_b, w_down):
    T = h.shape[1]
    u = h @ w_up
    up = jnp.pad(u, ((0, 0), (1, 1), (0, 0)))
    u = up[:, :T] * conv_w[0] + up[:, 1:T + 1] * conv_w[1] + up[:, 2:] * conv_w[2] + conv_b
    a, g = jnp.split(u, 2, axis=-1)
    return (jax.nn.silu(g) * a) @ w_down


def _trunk_layer(x, mods, attend, w_in, w_pool, pool_scale, w_out, ln1_g, ln1_b,
                 w_up, conv_w, conv_b, w_down, ln2_g, ln2_b):
    sh1, sc1, g1, sh2, sc2, g2 = mods
    h = x * (1.0 + sc1) + sh1
    q, k, v, p = _project(h, w_in)
    attn = attend(q, k, v)
    pool = _pool_mixer(p, w_pool, pool_scale)
    mix = jnp.concatenate([attn, pool], axis=-1) @ w_out
    x = _layer_norm(DEEPNORM_ALPHA * x + g1 * mix, ln1_g, ln1_b)
    h = x * (1.0 + sc2) + sh2
    ff = _conv_ffn(h, w_up, conv_w, conv_b, w_down)
    x = _layer_norm(DEEPNORM_ALPHA * x + g2 * ff, ln2_g, ln2_b)
    return x, k, v


def setup_inputs(seed: int = 0) -> dict:
    key = jax.random.key(seed)
    ks = jax.random.split(key, 24)
    f32 = jnp.float32

    def nrm(k, shape, scale):
        return jax.random.normal(k, shape, f32) * scale

    x_prompt = nrm(ks[0], (BATCH, SEQ, D_MODEL), 1.0)
    x_sample = nrm(ks[1], (DEC_BATCH, DEC_SEQ, D_MODEL), 1.0)
    cache_k = nrm(ks[2], (DEC_BATCH, DEPTH, PAST_LEN, N_KV_HEADS, HEAD_DIM), 1.0)
    cache_v = nrm(ks[3], (DEC_BATCH, DEPTH, PAST_LEN, N_KV_HEADS, HEAD_DIM), DEEPNORM_BETA)
    c = nrm(ks[4], (DEC_BATCH, D_MODEL), 1.0)
    c_ctx = nrm(ks[5], (D_MODEL,), 1.0)
    w_mod = nrm(ks[6], (DEPTH, D_MODEL, 6 * D_MODEL), 0.5 * D_MODEL ** -0.5)
    b_mod = nrm(ks[7], (DEPTH, 6 * D_MODEL), 0.01)
    cols = jnp.arange(IN_W)
    v_cols = (cols >= ATTN_W + KV_W) & (cols < ATTN_W + 2 * KV_W)
    w_in = nrm(ks[8], (DEPTH, D_MODEL, IN_W), D_MODEL ** -0.5) * jnp.where(v_cols, DEEPNORM_BETA, 1.0)
    attn_sink = nrm(ks[9], (DEPTH, N_HEADS), 0.5)
    w_pool = nrm(ks[10], (DEPTH, N_POOL_GROUPS, POOL_GROUP_W, POOL_GROUP_W), POOL_GROUP_W ** -0.5)
    pool_scale = 1.0 + nrm(ks[11], (DEPTH, POOL_W), 0.1)
    w_out = nrm(ks[12], (DEPTH, MIX_W, D_MODEL), DEEPNORM_BETA * MIX_W ** -0.5)
    ln1_g = 1.0 + nrm(ks[13], (DEPTH, D_MODEL), 0.05)
    ln1_b = nrm(ks[14], (DEPTH, D_MODEL), 0.02)
    w_up = nrm(ks[15], (DEPTH, D_MODEL, 2 * D_FF), D_MODEL ** -0.5)
    conv_w = jnp.array([0.0, 1.0, 0.0], f32)[None, :, None] + nrm(ks[16], (DEPTH, CONV_W, 2 * D_FF), 0.3)
    conv_b = nrm(ks[17], (DEPTH, 2 * D_FF), 0.02)
    w_down = nrm(ks[18], (DEPTH, D_FF, D_MODEL), DEEPNORM_BETA * D_FF ** -0.5)
    ln2_g = 1.0 + nrm(ks[19], (DEPTH, D_MODEL), 0.05)
    ln2_b = nrm(ks[20], (DEPTH, D_MODEL), 0.02)
    return {'x_prompt': x_prompt, 'x_sample': x_sample, 'cache_k': cache_k, 'cache_v': cache_v,
            'c': c, 'c_ctx': c_ctx, 'w_mod': w_mod, 'b_mod': b_mod, 'w_in': w_in,
            'attn_sink': attn_sink, 'w_pool': w_pool, 'pool_scale': pool_scale, 'w_out': w_out,
            'ln1_g': ln1_g, 'ln1_b': ln1_b, 'w_up': w_up, 'conv_w': conv_w, 'conv_b': conv_b,
            'w_down': w_down, 'ln2_g': ln2_g, 'ln2_b': ln2_b}


def reference(x_prompt, x_sample, cache_k, cache_v, c, c_ctx, w_mod, b_mod, w_in, attn_sink,
              w_pool, pool_scale, w_out, ln1_g, ln1_b, w_up, conv_w, conv_b, w_down, ln2_g, ln2_b):
    x = x_prompt
    ks, vs = [], []
    for l in range(DEPTH):
        mods = _modulation(c_ctx, w_mod[l], b_mod[l])
        sink_l = attn_sink[l]
        attend = lambda q, k, v, s=sink_l: _context_attention(q, k, v, s)
        x, k, v = _trunk_layer(x, mods, attend, w_in[l], w_pool[l], pool_scale[l], w_out[l],
                               ln1_g[l], ln1_b[l], w_up[l], conv_w[l], conv_b[l], w_down[l],
                               ln2_g[l], ln2_b[l])
        ks.append(k)
        vs.append(v)
    y_prompt = x
    new_cache_k = jnp.stack(ks, axis=1)
    new_cache_v = jnp.stack(vs, axis=1)

    x = x_sample
    for l in range(DEPTH):
        mods = [m[:, None, :] for m in _modulation(c, w_mod[l], b_mod[l])]
        sink_l = attn_sink[l]
        kc = cache_k[:, l]
        vc = cache_v[:, l]
        attend = lambda q, k, v, s=sink_l, kc=kc, vc=vc: _latent_attention(
            _axial_rope(q), _axial_rope(k), v, kc, vc, s)
        x, _, _ = _trunk_layer(x, mods, attend, w_in[l], w_pool[l], pool_scale[l], w_out[l],
                               ln1_g[l], ln1_b[l], w_up[l], conv_w[l], conv_b[l], w_down[l],
                               ln2_g[l], ln2_b[l])
    y_sample = x
    return (y_prompt, y_sample, new_cache_k, new_cache_v)
```

```python
import functools

import numpy as np
import jax
import jax.numpy as jnp
from jax import lax
from jax.experimental import pallas as pl
from jax.experimental.pallas import tpu as pltpu

D_MODEL = 1024
DEPTH = 4
N_HEADS = 8
N_KV_HEADS = 2
HEAD_DIM = 64
GQA_GROUP = N_HEADS // N_KV_HEADS
ATTN_W = N_HEADS * HEAD_DIM
KV_W = N_KV_HEADS * HEAD_DIM
POOL_WINDOWS = (2, 4, 8, 16)
N_POOL_GROUPS = 4
POOL_W = D_MODEL // 2
POOL_GROUP_W = POOL_W // N_POOL_GROUPS
MIX_W = ATTN_W + POOL_W
IN_W = ATTN_W + 2 * KV_W + POOL_W
D_FF = 2816
GRID_W = 64
WINDOW = 128
ROPE_BASE = 10000.0
LN_EPS = 1e-5
DEEPNORM_ALPHA = (2 * DEPTH) ** 0.25
ATTN_SCALE = HEAD_DIM ** -0.5
NEG_INF = -1e30

LANES = 128
TILE = 1024
FF_CHUNK = 256
N_FF_CHUNKS = D_FF // FF_CHUNK
MOD_COLS = 1536
MOD_ROWS = 16
VMEM_LIMIT = 56 * 1024 * 1024

F32 = jnp.float32
BF16 = jnp.bfloat16


def _silu(x):
    return x / (1.0 + jnp.exp(-x))


def _layer_norm(y, g, b):
    mu = jnp.mean(y, axis=-1, keepdims=True)
    yc = y - mu
    var = jnp.mean(yc * yc, axis=-1, keepdims=True)
    return yc * lax.rsqrt(var + LN_EPS) * g + b


def _mod_slice(mod_ref, j):
    return mod_ref[:, j * D_MODEL:(j + 1) * D_MODEL]


def _mod_kernel(cond_ref, w_ref, b_ref, o_ref):
    a = _silu(cond_ref[...])
    o_ref[0] = jnp.dot(a, w_ref[0], precision=lax.Precision.HIGHEST,
                       preferred_element_type=F32) + b_ref[0]


def _modulation(cond, w_mod, b_mod):
    n_col = (6 * D_MODEL) // MOD_COLS
    return pl.pallas_call(
        _mod_kernel,
        grid=(DEPTH, n_col),
        in_specs=[
            pl.BlockSpec((MOD_ROWS, D_MODEL), lambda l, j: (0, 0)),
            pl.BlockSpec((1, D_MODEL, MOD_COLS), lambda l, j: (l, 0, j)),
            pl.BlockSpec((1, 1, MOD_COLS), lambda l, j: (l, 0, j)),
        ],
        out_specs=pl.BlockSpec((1, MOD_ROWS, MOD_COLS), lambda l, j: (l, 0, j)),
        out_shape=jax.ShapeDtypeStruct((DEPTH, MOD_ROWS, 6 * D_MODEL), F32),
        compiler_params=pltpu.CompilerParams(
            dimension_semantics=("arbitrary", "arbitrary"), vmem_limit_bytes=VMEM_LIMIT),
        name="modulation",
    )(cond, w_mod, b_mod.reshape(DEPTH, 1, 6 * D_MODEL))


def _dup_halves(x):
    lo = lax.broadcasted_iota(jnp.int32, x.shape, 1) < HEAD_DIM
    xr = pltpu.roll(x, HEAD_DIM, axis=1)
    return jnp.where(lo, x, xr), jnp.where(lo, xr, x)


def _softmax_attend(lhs, sink_col, key_sets):
    scores = []
    m = sink_col
    for k2, _, bias in key_sets:
        s = lax.dot_general(lhs, k2, (((1,), (1,)), ((), ())), preferred_element_type=F32)
        if bias is not None:
            s = s + bias
        scores.append(s)
        m = jnp.maximum(m, jnp.max(s, axis=-1, keepdims=True))
    denom = jnp.exp(sink_col - m)
    out = None
    for s, (_, v2, _) in zip(scores, key_sets):
        e = jnp.exp(s - m)
        denom = denom + jnp.sum(e, axis=-1, keepdims=True)
        o = jnp.dot(e.astype(BF16), v2, preferred_element_type=F32)
        out = o if out is None else out + o
    return out / denom


def _mixer_kernel(*refs, latent, seq):
    if latent:
        (sink_ref, x_ref, mod_ref, w_in_ref, w_pool_ref, ps_ref, w_out_ref, g_ref, b_ref,
         kc_ref, vc_ref, rope_ref, x1_ref, q_s, k_s, v_s, mix_s, kc_s, vc_s) = refs
    else:
        (sink_ref, x_ref, mod_ref, w_in_ref, w_pool_ref, ps_ref, w_out_ref, g_ref, b_ref,
         x1_ref, k_out_ref, v_out_ref, q_s, k_s, v_s, mix_s) = refs

    x = x_ref[...]
    h = (x * (1.0 + _mod_slice(mod_ref, 1)) + _mod_slice(mod_ref, 0)).astype(BF16)
    qkvp = jnp.dot(h, w_in_ref[...], preferred_element_type=F32)
    k = qkvp[:, ATTN_W:ATTN_W + KV_W]
    v = qkvp[:, ATTN_W + KV_W:ATTN_W + 2 * KV_W]

    if latent:
        cos = rope_ref[0]
        sin_up = rope_ref[1]
        sin_dn = rope_ref[2]

        def rope(t):
            return (t * cos + pltpu.roll(t, LANES - HEAD_DIM // 4, axis=1) * sin_up
                    + pltpu.roll(t, HEAD_DIM // 4, axis=1) * sin_dn)

        k = rope(k)
        pad = WINDOW
        zeros = jnp.zeros((pad, LANES), BF16)
        for hh in range(N_KV_HEADS):
            k_s[hh, 0:pad, :] = zeros
            k_s[hh, pad + TILE:2 * pad + TILE, :] = zeros
            v_s[hh, 0:pad, :] = zeros
            v_s[hh, pad + TILE:2 * pad + TILE, :] = zeros
        kc0, kc1 = _dup_halves(kc_ref[...])
        vc0, vc1 = _dup_halves(vc_ref[...])
        kc_s[0] = kc0.astype(BF16)
        kc_s[1] = kc1.astype(BF16)
        vc_s[0] = vc0.astype(BF16)
        vc_s[1] = vc1.astype(BF16)
    else:
        pad = 0
        k_out_ref[...] = k
        v_out_ref[...] = v

    for j in range(ATTN_W // LANES):
        qj = qkvp[:, j * LANES:(j + 1) * LANES]
        if latent:
            qj = rope(qj)
        q_s[:, j * LANES:(j + 1) * LANES] = (qj * ATTN_SCALE).astype(BF16)
    k0, k1 = _dup_halves(k)
    v0, v1 = _dup_halves(v)
    k_s[0, pad:pad + TILE, :] = k0.astype(BF16)
    k_s[1, pad:pad + TILE, :] = k1.astype(BF16)
    v_s[0, pad:pad + TILE, :] = v0.astype(BF16)
    v_s[1, pad:pad + TILE, :] = v1.astype(BF16)

    qb = WINDOW if latent else seq
    n_rows = GQA_GROUP * qb
    row = lax.broadcasted_iota(jnp.int32, (n_rows, 1), 0)
    lo_q = lax.broadcasted_iota(jnp.int32, (qb, LANES), 1) < HEAD_DIM

    def attend_block(i, carry):
        r0 = pl.multiple_of(i * qb, qb)
        if latent:
            r = lax.broadcasted_iota(jnp.int32, (n_rows, 3 * WINDOW), 0) & (WINDOW - 1)
            c = lax.broadcasted_iota(jnp.int32, (n_rows, 3 * WINDOW), 1)
            qpos = i * WINDOW + r
            kpos = (i - 1) * WINDOW + c
            valid = (jnp.abs(qpos - kpos) <= WINDOW) & (kpos >= 0) & (kpos < TILE)
            bias = jnp.where(valid, 0.0, NEG_INF).astype(F32)
        for hh in range(N_KV_HEADS):
            parts = []
            for pair in range(GQA_GROUP // 2):
                c0 = hh * GQA_GROUP * HEAD_DIM + pair * LANES
                qp = q_s[pl.ds(r0, qb), c0:c0 + LANES]
                parts.append(jnp.where(lo_q, qp, jnp.zeros_like(qp)))
                parts.append(jnp.where(lo_q, jnp.zeros_like(qp), qp))
            lhs = jnp.concatenate(parts, axis=0)
            sink_col = jnp.full((n_rows, 1), sink_ref[hh * GQA_GROUP + GQA_GROUP - 1], F32)
            for g in range(GQA_GROUP - 2, -1, -1):
                sink_col = jnp.where(row < (g + 1) * qb, sink_ref[hh * GQA_GROUP + g], sink_col)
            if latent:
                key_sets = [(kc_s[hh], vc_s[hh], None),
                            (k_s[hh, pl.ds(r0, 3 * WINDOW), :], v_s[hh, pl.ds(r0, 3 * WINDOW), :], bias)]
            else:
                key_sets = [(k_s[hh, pl.ds(r0, qb), :], v_s[hh, pl.ds(r0, qb), :], None)]
            o = _softmax_attend(lhs, sink_col, key_sets)
            for pair in range(GQA_GROUP // 2):
                c0 = hh * GQA_GROUP * HEAD_DIM + pair * LANES
                oa = o[(2 * pair) * qb:(2 * pair + 1) * qb]
                ob = o[(2 * pair + 1) * qb:(2 * pair + 2) * qb]
                mix_s[pl.ds(r0, qb), c0:c0 + LANES] = jnp.where(lo_q, oa, ob).astype(BF16)
        return carry

    lax.fori_loop(0, TILE // qb, attend_block, 0)

    pos = lax.broadcasted_iota(jnp.int32, (TILE, POOL_GROUP_W), 0) & (seq - 1)
    for g in range(N_POOL_GROUPS):
        c0 = ATTN_W + 2 * KV_W + g * POOL_GROUP_W
        pg = qkvp[:, c0:c0 + POOL_GROUP_W]
        w = POOL_WINDOWS[g]
        total = pg
        for j in range(-(w // 2), w - w // 2):
            if j == 0:
                continue
            shifted = pltpu.roll(pg, (-j) % TILE, axis=0)
            ok = (pos >= -j) if j < 0 else (pos < seq - j)
            total = total + jnp.where(ok, shifted, 0.0)
        cnt = (jnp.minimum(pos + (w - w // 2), seq) - jnp.maximum(pos - w // 2, 0)).astype(F32)
        d = (total / cnt - pg).astype(BF16)
        y = jnp.dot(d, w_pool_ref[g], preferred_element_type=F32)
        y = y * ps_ref[:, g * POOL_GROUP_W:(g + 1) * POOL_GROUP_W]
        mix_s[:, ATTN_W + g * POOL_GROUP_W:ATTN_W + (g + 1) * POOL_GROUP_W] = y.astype(BF16)

    mix = jnp.dot(mix_s[...], w_out_ref[...], preferred_element_type=F32)
    y = DEEPNORM_ALPHA * x + _mod_slice(mod_ref, 2) * mix
    x1_ref[...] = _layer_norm(y, g_ref[...], b_ref[...])


def _const_spec(shape):
    zeros = (0,) * len(shape)
    return pl.BlockSpec(shape, lambda i: zeros, pipeline_mode=pl.Buffered(1))


def _mixer_call(x, mods, mod_row, sink, w_in, w_pool, pool_scale, w_out, ln_g, ln_b, *,
                latent, seq, cache_k=None, cache_v=None, rope=None, layer=None):
    n_tok = x.shape[0]
    n_tiles = n_tok // TILE
    in_specs = [
        pl.BlockSpec(memory_space=pltpu.SMEM),
        pl.BlockSpec((TILE, D_MODEL), lambda i: (i, 0)),
        pl.BlockSpec((None, 1, 6 * D_MODEL), lambda i: (mod_row(i), 0, 0)),
        _const_spec((D_MODEL, IN_W)),
        _const_spec((N_POOL_GROUPS, POOL_GROUP_W, POOL_GROUP_W)),
        _const_spec((1, POOL_W)),
        _const_spec((MIX_W, D_MODEL)),
        _const_spec((1, D_MODEL)),
        _const_spec((1, D_MODEL)),
    ]
    args = [sink, x, mods, w_in, w_pool, pool_scale, w_out, ln_g, ln_b]
    pad = WINDOW if latent else 0
    scratch = [
        pltpu.VMEM((TILE, ATTN_W), BF16),
        pltpu.VMEM((N_KV_HEADS, TILE + 2 * pad, LANES), BF16),
        pltpu.VMEM((N_KV_HEADS, TILE + 2 * pad, LANES), BF16),
        pltpu.VMEM((TILE, MIX_W), BF16),
    ]
    x_out = jax.ShapeDtypeStruct((n_tok, D_MODEL), F32)
    x_spec = pl.BlockSpec((TILE, D_MODEL), lambda i: (i, 0))
    if latent:
        past = cache_k.shape[2]
        in_specs += [
            pl.BlockSpec((None, None, past, KV_W), lambda i: (i, layer, 0, 0)),
            pl.BlockSpec((None, None, past, KV_W), lambda i: (i, layer, 0, 0)),
            _const_spec((3, TILE, LANES)),
        ]
        args += [cache_k, cache_v, rope]
        scratch += [pltpu.VMEM((N_KV_HEADS, past, LANES), BF16),
                    pltpu.VMEM((N_KV_HEADS, past, LANES), BF16)]
        out_shape = x_out
        out_specs = x_spec
    else:
        kv_out = jax.ShapeDtypeStruct((n_tok, KV_W), F32)
        kv_spec = pl.BlockSpec((TILE, KV_W), lambda i: (i, 0))
        out_shape = (x_out, kv_out, kv_out)
        out_specs = (x_spec, kv_spec, kv_spec)
    return pl.pallas_call(
        functools.partial(_mixer_kernel, latent=latent, seq=seq),
        grid=(n_tiles,),
        in_specs=in_specs,
        out_specs=out_specs,
        out_shape=out_shape,
        scratch_shapes=scratch,
        compiler_params=pltpu.CompilerParams(
            dimension_semantics=("arbitrary",), vmem_limit_bytes=VMEM_LIMIT),
        name="mixer_latent" if latent else "mixer_context",
    )(*args)


def _ffn_kernel(x_ref, mod_ref, w_up_ref, cw_ref, cb_ref, w_down_ref, g_ref, b_ref, o_ref,
                h_s, act_s, *, seq):
    x = x_ref[...]
    h_s[...] = (x * (1.0 + _mod_slice(mod_ref, 4)) + _mod_slice(mod_ref, 3)).astype(BF16)
    pos = lax.broadcasted_iota(jnp.int32, (TILE, FF_CHUNK), 0) & (seq - 1)
    has_prev = pos != 0
    has_next = pos != seq - 1

    def conv(u, c):
        cw = cw_ref[c]
        prev = jnp.where(has_prev, pltpu.roll(u, 1, axis=0), 0.0)
        nxt = jnp.where(has_next, pltpu.roll(u, TILE - 1, axis=0), 0.0)
        return prev * cw[0:1] + u * cw[1:2] + nxt * cw[2:3] + cb_ref[c]

    def chunk(c, carry):
        hb = h_s[...]
        a = conv(jnp.dot(hb, w_up_ref[c], preferred_element_type=F32), c)
        g = conv(jnp.dot(hb, w_up_ref[N_FF_CHUNKS + c], preferred_element_type=F32),
                 N_FF_CHUNKS + c)
        act_s[c] = (_silu(g) * a).astype(BF16)
        return carry

    lax.fori_loop(0, N_FF_CHUNKS, chunk, 0)

    ff = jnp.dot(act_s[0], w_down_ref[0], preferred_element_type=F32)
    for c in range(1, N_FF_CHUNKS):
        ff = ff + jnp.dot(act_s[c], w_down_ref[c], preferred_element_type=F32)
    y = DEEPNORM_ALPHA * x + _mod_slice(mod_ref, 5) * ff
    o_ref[...] = _layer_norm(y, g_ref[...], b_ref[...])


def _ffn_call(x, mods, mod_row, w_up, conv_w, conv_b, w_down, ln_g, ln_b, *, seq, name):
    n_tok = x.shape[0]
    return pl.pallas_call(
        functools.partial(_ffn_kernel, seq=seq),
        grid=(n_tok // TILE,),
        in_specs=[
            pl.BlockSpec((TILE, D_MODEL), lambda i: (i, 0)),
            pl.BlockSpec((None, 1, 6 * D_MODEL), lambda i: (mod_row(i), 0, 0)),
            _const_spec((2 * N_FF_CHUNKS, D_MODEL, FF_CHUNK)),
            _const_spec((2 * N_FF_CHUNKS, 3, FF_CHUNK)),
            _const_spec((2 * N_FF_CHUNKS, 1, FF_CHUNK)),
            _const_spec((N_FF_CHUNKS, FF_CHUNK, D_MODEL)),
            _const_spec((1, D_MODEL)),
            _const_spec((1, D_MODEL)),
        ],
        out_specs=pl.BlockSpec((TILE, D_MODEL), lambda i: (i, 0)),
        out_shape=jax.ShapeDtypeStruct((n_tok, D_MODEL), F32),
        scratch_shapes=[pltpu.VMEM((TILE, D_MODEL), BF16),
                        pltpu.VMEM((N_FF_CHUNKS, TILE, FF_CHUNK), BF16)],
        compiler_params=pltpu.CompilerParams(
            dimension_semantics=("arbitrary",), vmem_limit_bytes=VMEM_LIMIT),
        name=name,
    )(x, mods, w_up, conv_w, conv_b, w_down, ln_g, ln_b)


def _rope_tables(n_pos):
    half = HEAD_DIM // 2
    t = jnp.arange(n_pos)
    inv_freq = ROPE_BASE ** (-jnp.arange(0, half, 2, dtype=F32) / half)

    def ang(p):
        a = p.astype(F32)[:, None] * inv_freq[None, :]
        return jnp.concatenate([a, a], axis=-1)

    a = jnp.concatenate([ang(t // GRID_W), ang(t % GRID_W)], axis=-1)
    cos = jnp.cos(a)
    sin = jnp.sin(a)
    first = (jnp.arange(HEAD_DIM) % half) < half // 2
    sin_up = jnp.where(first[None, :], -sin, 0.0)
    sin_dn = jnp.where(first[None, :], 0.0, sin)
    tabs = jnp.stack([cos, sin_up, sin_dn])
    return jnp.concatenate([tabs, tabs], axis=-1).astype(F32)


def kernel(x_prompt, x_sample, cache_k, cache_v, c, c_ctx, w_mod, b_mod, w_in, attn_sink,
           w_pool, pool_scale, w_out, ln1_g, ln1_b, w_up, conv_w, conv_b, w_down, ln2_g, ln2_b):
    batch, seq, _ = x_prompt.shape
    dec_batch, dec_seq, _ = x_sample.shape
    past = cache_k.shape[2]
    assert TILE % seq == 0 and dec_seq == TILE and seq & (seq - 1) == 0
    assert (batch * seq) % TILE == 0 and past % 16 == 0

    cond = jnp.zeros((MOD_ROWS, D_MODEL), F32).at[0].set(c_ctx).at[1:1 + dec_batch].set(c)
    mods = _modulation(cond, w_mod, b_mod).reshape(DEPTH, MOD_ROWS, 1, 6 * D_MODEL)

    w_in_b = w_in.astype(BF16)
    w_pool_b = w_pool.astype(BF16)
    w_out_b = w_out.astype(BF16)
    w_up_b = w_up.astype(BF16).reshape(DEPTH, D_MODEL, 2 * N_FF_CHUNKS, FF_CHUNK).transpose(0, 2, 1, 3)
    w_down_b = w_down.astype(BF16).reshape(DEPTH, N_FF_CHUNKS, FF_CHUNK, D_MODEL)
    conv_w_r = conv_w.reshape(DEPTH, 3, 2 * N_FF_CHUNKS, FF_CHUNK).transpose(0, 2, 1, 3)
    conv_b_r = conv_b.reshape(DEPTH, 2 * N_FF_CHUNKS, 1, FF_CHUNK)
    rope = _rope_tables(dec_seq)
    cache_k_r = cache_k.reshape(dec_batch, DEPTH, past, KV_W)
    cache_v_r = cache_v.reshape(dec_batch, DEPTH, past, KV_W)

    xc = x_prompt.reshape(batch * seq, D_MODEL)
    xl = x_sample.reshape(dec_batch * dec_seq, D_MODEL)
    ctx_row = lambda i: 0
    lat_row = lambda i: i + 1
    ks, vs = [], []
    for l in range(DEPTH):
        mixer_w = (attn_sink[l], w_in_b[l], w_pool_b[l], pool_scale[l][None], w_out_b[l],
                   ln1_g[l][None], ln1_b[l][None])
        ffn_w = (w_up_b[l], conv_w_r[l], conv_b_r[l], w_down_b[l], ln2_g[l][None], ln2_b[l][None])
        xc, k_l, v_l = _mixer_call(xc, mods[l], ctx_row, *mixer_w, latent=False, seq=seq)
        xc = _ffn_call(xc, mods[l], ctx_row, *ffn_w, seq=seq, name="ffn_context")
        xl = _mixer_call(xl, mods[l], lat_row, *mixer_w, latent=True, seq=dec_seq,
                         cache_k=cache_k_r, cache_v=cache_v_r, rope=rope, layer=l)
        xl = _ffn_call(xl, mods[l], lat_row, *ffn_w, seq=dec_seq, name="ffn_latent")
        ks.append(k_l.reshape(batch, seq, N_KV_HEADS, HEAD_DIM))
        vs.append(v_l.reshape(batch, seq, N_KV_HEADS, HEAD_DIM))
    y_prompt = xc.reshape(batch, seq, D_MODEL)
    y_sample = xl.reshape(dec_batch, dec_seq, D_MODEL)
    return (y_prompt, y_sample, jnp.stack(ks, axis=1), jnp.stack(vs, axis=1))
```

```python
import functools

import jax
import jax.numpy as jnp
from jax import lax
from jax.experimental import pallas as pl
from jax.experimental.pallas import tpu as pltpu

D_MODEL = 1024
DEPTH = 4
N_HEADS = 8
N_KV_HEADS = 2
HEAD_DIM = 64
GQA_GROUP = N_HEADS // N_KV_HEADS
ATTN_W = N_HEADS * HEAD_DIM
KV_W = N_KV_HEADS * HEAD_DIM
POOL_WINDOWS = (2, 4, 8, 16)
N_POOL_GROUPS = 4
POOL_W = D_MODEL // 2
POOL_GROUP_W = POOL_W // N_POOL_GROUPS
MIX_W = ATTN_W + POOL_W
IN_W = ATTN_W + 2 * KV_W + POOL_W
D_FF = 2816
GRID_W = 64
WINDOW = 128
ROPE_BASE = 10000.0
LN_EPS = 1e-5
DEEPNORM_ALPHA = (2 * DEPTH) ** 0.25
ATTN_SCALE = HEAD_DIM ** -0.5
NEG_INF = -1e30

LANES = 128
SUBLANES = 8
TILE = 1024
FF_CHUNK = 256
N_FF_CHUNKS = D_FF // FF_CHUNK
EPI_ROWS = 512
MOD_COLS = 1536
MOD_ROWS = 16
VMEM_LIMIT = 60000 * 1024

F32 = jnp.float32
BF16 = jnp.bfloat16


def _silu(x):
    return x / (1.0 + jnp.exp(-x))


def _layer_norm(y, g, b):
    mu = jnp.mean(y, axis=-1, keepdims=True)
    yc = y - mu
    var = jnp.mean(yc * yc, axis=-1, keepdims=True)
    return yc * lax.rsqrt(var + LN_EPS) * g + b


def _mod_slice(mod_ref, j):
    return mod_ref[:, j * D_MODEL:(j + 1) * D_MODEL]


COL_BLOCKS = D_MODEL // LANES


def _cols_load(ref, r0, n):
    return jnp.concatenate([ref[j, r0:r0 + n, :] for j in range(COL_BLOCKS)], axis=1)


def _cols_store(ref, r0, y):
    for j in range(COL_BLOCKS):
        ref[j, r0:r0 + y.shape[0], :] = y[:, j * LANES:(j + 1) * LANES]


def _load_natural(ref, seq):
    per = seq // SUBLANES
    cols = []
    for j in range(COL_BLOCKS):
        cols.append(jnp.concatenate(
            [ref[j, pl.ds(s * seq + a, per, stride=SUBLANES), :]
             for s in range(TILE // seq) for a in range(SUBLANES)], axis=0))
    return jnp.concatenate(cols, axis=1)


def _store_permuted(ref, y, seq):
    per = seq // SUBLANES
    for j in range(COL_BLOCKS):
        for s in range(TILE // seq):
            for a in range(SUBLANES):
                r0 = s * seq + a * per
                ref[j, pl.ds(s * seq + a, per, stride=SUBLANES), :] = (
                    y[r0:r0 + per, j * LANES:(j + 1) * LANES])


def _store_natural(o_ref, stage_ref, y, row0, seq):
    n = y.shape[0]
    per = seq // SUBLANES
    _cols_store(stage_ref, 0, y)
    span = min(n, seq)
    for q0 in range(0, n, span):
        s, b0 = (row0 + q0) // seq, ((row0 + q0) % seq) // SUBLANES
        for a in range(SUBLANES):
            t0 = s * seq + a * per + b0
            for j in range(COL_BLOCKS):
                o_ref[t0:t0 + span // SUBLANES, j * LANES:(j + 1) * LANES] = (
                    stage_ref[j, pl.ds(q0 + a, span // SUBLANES, stride=SUBLANES), :])


def _cols_spec():
    return pl.BlockSpec((COL_BLOCKS, TILE, LANES), lambda i: (0, i, 0))


def _layer_spec(shape, layer):
    idx = (layer,) + (0,) * len(shape)
    return pl.BlockSpec((None,) + tuple(shape), lambda i: idx, pipeline_mode=pl.Buffered(1))


def _mod_kernel(cond_ref, w_ref, b_ref, o_ref):
    a = _silu(cond_ref[...]).astype(BF16)
    o_ref[0] = jnp.dot(a, w_ref[0].astype(BF16), preferred_element_type=F32) + b_ref[0]


def _modulation(cond, w_mod, b_mod):
    n_col = (6 * D_MODEL) // MOD_COLS
    return pl.pallas_call(
        _mod_kernel,
        grid=(DEPTH, n_col),
        in_specs=[
            pl.BlockSpec((MOD_ROWS, D_MODEL), lambda l, j: (0, 0)),
            pl.BlockSpec((1, D_MODEL, MOD_COLS), lambda l, j: (l, 0, j)),
            pl.BlockSpec((1, 1, MOD_COLS), lambda l, j: (l, 0, j)),
        ],
        out_specs=pl.BlockSpec((1, MOD_ROWS, MOD_COLS), lambda l, j: (l, 0, j)),
        out_shape=jax.ShapeDtypeStruct((DEPTH, MOD_ROWS, 6 * D_MODEL), F32),
        compiler_params=pltpu.CompilerParams(
            dimension_semantics=("arbitrary", "arbitrary"), vmem_limit_bytes=VMEM_LIMIT),
        name="modulation",
    )(cond, w_mod, b_mod.reshape(DEPTH, 1, 6 * D_MODEL))


def _dup_halves(x):
    lo = lax.broadcasted_iota(jnp.int32, x.shape, 1) < HEAD_DIM
    xr = pltpu.roll(x, HEAD_DIM, axis=1)
    return jnp.where(lo, x, xr), jnp.where(lo, xr, x)


def _softmax_attend(lhs, sink_col, key_sets):
    scores = []
    m = sink_col
    for k2, _, bias in key_sets:
        s = lax.dot_general(lhs, k2, (((1,), (1,)), ((), ())), preferred_element_type=F32)
        if bias is not None:
            rows, cols = s.shape
            s = (s.reshape(GQA_GROUP, rows // GQA_GROUP, cols) + bias[None]).reshape(rows, cols)
        scores.append(s)
        m = jnp.maximum(m, jnp.max(s, axis=-1, keepdims=True))
    denom = jnp.exp(sink_col - m)
    out = None
    for s, (_, v2, _) in zip(scores, key_sets):
        e = jnp.exp(s - m)
        denom = denom + jnp.sum(e, axis=-1, keepdims=True)
        o = jnp.dot(e.astype(BF16), v2, preferred_element_type=F32)
        out = o if out is None else out + o
    return out / denom


def _mixer_kernel(*refs, latent, seq, layer, x_permuted):
    if latent:
        (sink_ref, x_ref, mod_ref, w_in_ref, w_pool_ref, ps_ref, w_out_ref, g_ref, b_ref,
         kc_ref, vc_ref, rope_ref, x1_ref, q_s, k_s, v_s, mix_s, kc_s, vc_s) = refs
    else:
        (sink_ref, x_ref, mod_ref, w_in_ref, w_pool_ref, ps_ref, w_out_ref, g_ref, b_ref,
         x1_ref, k_out_ref, v_out_ref, q_s, k_s, v_s, mix_s) = refs

    x = _load_natural(x_ref, seq) if x_permuted else x_ref[...]
    h = (x * (1.0 + _mod_slice(mod_ref, 1)) + _mod_slice(mod_ref, 0)).astype(BF16)
    qkvp = jnp.dot(h, w_in_ref[...], preferred_element_type=F32)
    k = qkvp[:, ATTN_W:ATTN_W + KV_W]
    v = qkvp[:, ATTN_W + KV_W:ATTN_W + 2 * KV_W]

    if latent:
        cos = rope_ref[0]
        sin_up = rope_ref[1]
        sin_dn = rope_ref[2]

        def rope(t):
            return (t * cos + pltpu.roll(t, LANES - HEAD_DIM // 4, axis=1) * sin_up
                    + pltpu.roll(t, HEAD_DIM // 4, axis=1) * sin_dn)

        k = rope(k)
        pad = WINDOW
        zeros = jnp.zeros((pad, LANES), BF16)
        for hh in range(N_KV_HEADS):
            k_s[hh, 0:pad, :] = zeros
            k_s[hh, pad + TILE:2 * pad + TILE, :] = zeros
            v_s[hh, 0:pad, :] = zeros
            v_s[hh, pad + TILE:2 * pad + TILE, :] = zeros
        kc0, kc1 = _dup_halves(kc_ref[...])
        vc0, vc1 = _dup_halves(vc_ref[...])
        kc_s[0] = kc0.astype(BF16)
        kc_s[1] = kc1.astype(BF16)
        vc_s[0] = vc0.astype(BF16)
        vc_s[1] = vc1.astype(BF16)
    else:
        pad = 0
        k_out_ref[...] = k
        v_out_ref[...] = v

    for j in range(ATTN_W // LANES):
        qj = qkvp[:, j * LANES:(j + 1) * LANES]
        if latent:
            qj = rope(qj)
        q_s[:, j * LANES:(j + 1) * LANES] = (qj * ATTN_SCALE).astype(BF16)
    k0, k1 = _dup_halves(k)
    v0, v1 = _dup_halves(v)
    k_s[0, pad:pad + TILE, :] = k0.astype(BF16)
    k_s[1, pad:pad + TILE, :] = k1.astype(BF16)
    v_s[0, pad:pad + TILE, :] = v0.astype(BF16)
    v_s[1, pad:pad + TILE, :] = v1.astype(BF16)

    qb = WINDOW if latent else seq
    n_rows = GQA_GROUP * qb
    row = lax.broadcasted_iota(jnp.int32, (n_rows, 1), 0)
    lo_q = lax.broadcasted_iota(jnp.int32, (qb, LANES), 1) < HEAD_DIM
    if latent:
        r = lax.broadcasted_iota(jnp.int32, (qb, 3 * WINDOW), 0)
        c = lax.broadcasted_iota(jnp.int32, (qb, 3 * WINDOW), 1)
        in_band = jnp.abs(r - c + WINDOW) <= WINDOW

    def attend_block(i, carry):
        r0 = pl.multiple_of(i * qb, qb)
        if latent:
            kpos = (i - 1) * WINDOW + c
            valid = in_band & (kpos >= 0) & (kpos < TILE)
            bias = jnp.where(valid, 0.0, NEG_INF).astype(F32)
        for hh in range(N_KV_HEADS):
            parts = []
            for pair in range(GQA_GROUP // 2):
                c0 = hh * GQA_GROUP * HEAD_DIM + pair * LANES
                qp = q_s[pl.ds(r0, qb), c0:c0 + LANES]
                parts.append(jnp.where(lo_q, qp, jnp.zeros_like(qp)))
                parts.append(jnp.where(lo_q, jnp.zeros_like(qp), qp))
            lhs = jnp.concatenate(parts, axis=0)
            h0 = hh * GQA_GROUP
            sink_col = jnp.full((n_rows, 1), sink_ref[layer, h0 + GQA_GROUP - 1], F32)
            for g in range(GQA_GROUP - 2, -1, -1):
                sink_col = jnp.where(row < (g + 1) * qb, sink_ref[layer, h0 + g], sink_col)
            if latent:
                key_sets = [(kc_s[hh], vc_s[hh], None),
                            (k_s[hh, pl.ds(r0, 3 * WINDOW), :], v_s[hh, pl.ds(r0, 3 * WINDOW), :], bias)]
            else:
                key_sets = [(k_s[hh, pl.ds(r0, qb), :], v_s[hh, pl.ds(r0, qb), :], None)]
            o = _softmax_attend(lhs, sink_col, key_sets)
            for pair in range(GQA_GROUP // 2):
                c0 = hh * GQA_GROUP * HEAD_DIM + pair * LANES
                oa = o[(2 * pair) * qb:(2 * pair + 1) * qb]
                ob = o[(2 * pair + 1) * qb:(2 * pair + 2) * qb]
                mix_s[pl.ds(r0, qb), c0:c0 + LANES] = jnp.where(lo_q, oa, ob).astype(BF16)
        return carry

    lax.fori_loop(0, TILE // qb, attend_block, 0)

    pos = lax.broadcasted_iota(jnp.int32, (TILE, POOL_GROUP_W), 0) & (seq - 1)
    for g in range(N_POOL_GROUPS):
        c0 = ATTN_W + 2 * KV_W + g * POOL_GROUP_W
        pg = qkvp[:, c0:c0 + POOL_GROUP_W]
        w = POOL_WINDOWS[g]
        total = pg
        for j in range(-(w // 2), w - w // 2):
            if j == 0:
                continue
            shifted = pltpu.roll(pg, (-j) % TILE, axis=0)
            ok = (pos >= -j) if j < 0 else (pos < seq - j)
            total = total + jnp.where(ok, shifted, 0.0)
        cnt = (jnp.minimum(pos + (w - w // 2), seq) - jnp.maximum(pos - w // 2, 0)).astype(F32)
        d = (total / cnt - pg).astype(BF16)
        y = jnp.dot(d, w_pool_ref[g], preferred_element_type=F32)
        y = y * ps_ref[:, g * POOL_GROUP_W:(g + 1) * POOL_GROUP_W]
        mix_s[:, ATTN_W + g * POOL_GROUP_W:ATTN_W + (g + 1) * POOL_GROUP_W] = y.astype(BF16)

    mix = jnp.dot(mix_s[...], w_out_ref[...], preferred_element_type=F32)
    y = DEEPNORM_ALPHA * x + _mod_slice(mod_ref, 2) * mix
    _store_permuted(x1_ref, _layer_norm(y, g_ref[...], b_ref[...]), seq)


def _mixer_call(x, mods, mod_row, sink, w_in, w_pool, pool_scale, w_out, ln_g, ln_b, *,
                layer, latent, seq, x_permuted, cache_k=None, cache_v=None, rope=None):
    n_tok = x.shape[1] if x_permuted else x.shape[0]
    n_tiles = n_tok // TILE
    in_specs = [
        pl.BlockSpec(memory_space=pltpu.SMEM),
        _cols_spec() if x_permuted else pl.BlockSpec((TILE, D_MODEL), lambda i: (i, 0)),
        pl.BlockSpec((None, None, 1, 6 * D_MODEL), lambda i: (layer, mod_row(i), 0, 0)),
        _layer_spec((D_MODEL, IN_W), layer),
        _layer_spec((N_POOL_GROUPS, POOL_GROUP_W, POOL_GROUP_W), layer),
        _layer_spec((1, POOL_W), layer),
        _layer_spec((MIX_W, D_MODEL), layer),
        _layer_spec((1, D_MODEL), layer),
        _layer_spec((1, D_MODEL), layer),
    ]
    args = [sink, x, mods, w_in, w_pool, pool_scale, w_out, ln_g, ln_b]
    pad = WINDOW if latent else 0
    scratch = [
        pltpu.VMEM((TILE, ATTN_W), BF16),
        pltpu.VMEM((N_KV_HEADS, TILE + 2 * pad, LANES), BF16),
        pltpu.VMEM((N_KV_HEADS, TILE + 2 * pad, LANES), BF16),
        pltpu.VMEM((TILE, MIX_W), BF16),
    ]
    x_out = jax.ShapeDtypeStruct((COL_BLOCKS, n_tok, LANES), F32)
    x_spec = _cols_spec()
    if latent:
        past = cache_k.shape[2]
        in_specs += [
            pl.BlockSpec((None, None, past, KV_W), lambda i: (i, layer, 0, 0)),
            pl.BlockSpec((None, None, past, KV_W), lambda i: (i, layer, 0, 0)),
            pl.BlockSpec((3, TILE, LANES), lambda i: (0, 0, 0), pipeline_mode=pl.Buffered(1)),
        ]
        args += [cache_k, cache_v, rope]
        scratch += [pltpu.VMEM((N_KV_HEADS, past, LANES), BF16),
                    pltpu.VMEM((N_KV_HEADS, past, LANES), BF16)]
        out_shape = x_out
        out_specs = x_spec
    else:
        kv_out = jax.ShapeDtypeStruct((n_tok, KV_W), F32)
        kv_spec = pl.BlockSpec((TILE, KV_W), lambda i: (i, 0))
        out_shape = (x_out, kv_out, kv_out)
        out_specs = (x_spec, kv_spec, kv_spec)
    return pl.pallas_call(
        functools.partial(_mixer_kernel, latent=latent, seq=seq, layer=layer,
                          x_permuted=x_permuted),
        grid=(n_tiles,),
        in_specs=in_specs,
        out_specs=out_specs,
        out_shape=out_shape,
        scratch_shapes=scratch,
        compiler_params=pltpu.CompilerParams(
            dimension_semantics=("arbitrary",), vmem_limit_bytes=VMEM_LIMIT),
        name="mixer_latent" if latent else "mixer_context",
    )(*args)


def _ffn_kernel(x_ref, mod_ref, w_up_ref, cw_ref, cb_ref, w_down_ref, g_ref, b_ref, o_ref,
                h_s, u_s, act_s, *stage, seq, natural_out):
    n_seq = TILE // seq
    h_s[...] = (_cols_load(x_ref, 0, TILE) * (1.0 + _mod_slice(mod_ref, 4))
                + _mod_slice(mod_ref, 3)).astype(BF16)
    sub = lax.broadcasted_iota(jnp.int32, (SUBLANES, FF_CHUNK), 0)

    def up_project(c):
        hb = h_s[...]
        for part in range(2):
            col = part * D_FF + c * FF_CHUNK
            u_s[c % 2, part] = jnp.dot(hb, w_up_ref[:, col:col + FF_CHUNK],
                                       preferred_element_type=F32)

    def conv_swiglu(c):
        for s in range(n_seq):
            halves = []
            for part in range(2):
                col = part * D_FF + c * FF_CHUNK
                cw = cw_ref[:, col:col + FF_CHUNK]
                cur = u_s[c % 2, part, s * seq:(s + 1) * seq, :]
                wrap_prev = jnp.where(sub == 0, 0.0, pltpu.roll(cur[seq - SUBLANES:], 1, axis=0))
                wrap_next = jnp.where(sub == SUBLANES - 1, 0.0,
                                      pltpu.roll(cur[:SUBLANES], SUBLANES - 1, axis=0))
                prev = jnp.concatenate([wrap_prev, cur[:seq - SUBLANES]], axis=0)
                nxt = jnp.concatenate([cur[SUBLANES:], wrap_next], axis=0)
                halves.append(prev * cw[0:1] + cur * cw[1:2] + nxt * cw[2:3]
                              + cb_ref[:, col:col + FF_CHUNK])
            a, g = halves
            act_s[s * seq:(s + 1) * seq, c * FF_CHUNK:(c + 1) * FF_CHUNK] = (_silu(g) * a).astype(BF16)

    up_project(0)
    for c in range(N_FF_CHUNKS):
        if c + 1 < N_FF_CHUNKS:
            up_project(c + 1)
        conv_swiglu(c)

    for r0 in range(0, TILE, EPI_ROWS):
        ff = jnp.dot(act_s[r0:r0 + EPI_ROWS, :], w_down_ref[...], preferred_element_type=F32)
        y = DEEPNORM_ALPHA * _cols_load(x_ref, r0, EPI_ROWS) + _mod_slice(mod_ref, 5) * ff
        out = _layer_norm(y, g_ref[...], b_ref[...])
        if natural_out:
            _store_natural(o_ref, stage[0], out, r0, seq)
        else:
            _cols_store(o_ref, r0, out)


def _ffn_call(x, mods, mod_row, w_up, conv_w, conv_b, w_down, ln_g, ln_b, *, layer, seq,
              natural_out, name):
    n_tok = x.shape[1]
    scratch = [pltpu.VMEM((TILE, D_MODEL), BF16),
               pltpu.VMEM((2, 2, TILE, FF_CHUNK), F32),
               pltpu.VMEM((TILE, D_FF), BF16)]
    if natural_out:
        scratch.append(pltpu.VMEM((COL_BLOCKS, EPI_ROWS, LANES), F32))
        out_specs = pl.BlockSpec((TILE, D_MODEL), lambda i: (i, 0))
        out_shape = jax.ShapeDtypeStruct((n_tok, D_MODEL), F32)
    else:
        out_specs = _cols_spec()
        out_shape = jax.ShapeDtypeStruct((COL_BLOCKS, n_tok, LANES), F32)
    return pl.pallas_call(
        functools.partial(_ffn_kernel, seq=seq, natural_out=natural_out),
        grid=(n_tok // TILE,),
        in_specs=[
            _cols_spec(),
            pl.BlockSpec((None, None, 1, 6 * D_MODEL), lambda i: (layer, mod_row(i), 0, 0)),
            _layer_spec((D_MODEL, 2 * D_FF), layer),
            _layer_spec((3, 2 * D_FF), layer),
            _layer_spec((1, 2 * D_FF), layer),
            _layer_spec((D_FF, D_MODEL), layer),
            _layer_spec((1, D_MODEL), layer),
            _layer_spec((1, D_MODEL), layer),
        ],
        out_specs=out_specs,
        out_shape=out_shape,
        scratch_shapes=scratch,
        compiler_params=pltpu.CompilerParams(
            dimension_semantics=("arbitrary",), vmem_limit_bytes=VMEM_LIMIT),
        name=name,
    )(x, mods, w_up, conv_w, conv_b, w_down, ln_g, ln_b)


def _rope_tables(n_pos):
    half = HEAD_DIM // 2
    t = jnp.arange(n_pos)
    inv_freq = ROPE_BASE ** (-jnp.arange(0, half, 2, dtype=F32) / half)

    def ang(p):
        a = p.astype(F32)[:, None] * inv_freq[None, :]
        return jnp.concatenate([a, a], axis=-1)

    a = jnp.concatenate([ang(t // GRID_W), ang(t % GRID_W)], axis=-1)
    cos = jnp.cos(a)
    sin = jnp.sin(a)
    first = (jnp.arange(HEAD_DIM) % half) < half // 2
    sin_up = jnp.where(first[None, :], -sin, 0.0)
    sin_dn = jnp.where(first[None, :], 0.0, sin)
    tabs = jnp.stack([cos, sin_up, sin_dn])
    return jnp.concatenate([tabs, tabs], axis=-1).astype(F32)


def kernel(x_prompt, x_sample, cache_k, cache_v, c, c_ctx, w_mod, b_mod, w_in, attn_sink,
           w_pool, pool_scale, w_out, ln1_g, ln1_b, w_up, conv_w, conv_b, w_down, ln2_g, ln2_b):
    batch, seq, _ = x_prompt.shape
    dec_batch, dec_seq, _ = x_sample.shape
    past = cache_k.shape[2]
    assert TILE % seq == 0 and dec_seq == TILE and seq & (seq - 1) == 0
    assert (batch * seq) % TILE == 0 and past % 16 == 0

    cond = jnp.zeros((MOD_ROWS, D_MODEL), F32).at[0].set(c_ctx).at[1:1 + dec_batch].set(c)
    mods = _modulation(cond, w_mod, b_mod).reshape(DEPTH, MOD_ROWS, 1, 6 * D_MODEL)

    mixer_w = (attn_sink, w_in.astype(BF16), w_pool.astype(BF16),
               pool_scale.reshape(DEPTH, 1, POOL_W), w_out.astype(BF16),
               ln1_g.reshape(DEPTH, 1, D_MODEL), ln1_b.reshape(DEPTH, 1, D_MODEL))
    ffn_w = (w_up.astype(BF16), conv_w, conv_b.reshape(DEPTH, 1, 2 * D_FF), w_down.astype(BF16),
             ln2_g.reshape(DEPTH, 1, D_MODEL), ln2_b.reshape(DEPTH, 1, D_MODEL))
    rope = _rope_tables(dec_seq)
    cache_k_r = cache_k.reshape(dec_batch, DEPTH, past, KV_W)
    cache_v_r = cache_v.reshape(dec_batch, DEPTH, past, KV_W)

    xc = x_prompt.reshape(batch * seq, D_MODEL)
    xl = x_sample.reshape(dec_batch * dec_seq, D_MODEL)
    ctx_row = lambda i: 0
    lat_row = lambda i: i + 1
    ks, vs = [], []
    for l in range(DEPTH):
        first, last = l == 0, l == DEPTH - 1
        xc, k_l, v_l = _mixer_call(xc, mods, ctx_row, *mixer_w, layer=l, latent=False, seq=seq,
                                   x_permuted=not first)
        xc = _ffn_call(xc, mods, ctx_row, *ffn_w, layer=l, seq=seq, natural_out=last,
                       name="ffn_context")
        xl = _mixer_call(xl, mods, lat_row, *mixer_w, layer=l, latent=True, seq=dec_seq,
                         x_permuted=not first, cache_k=cache_k_r, cache_v=cache_v_r, rope=rope)
        xl = _ffn_call(xl, mods, lat_row, *ffn_w, layer=l, seq=dec_seq, natural_out=last,
                       name="ffn_latent")
        ks.append(k_l.reshape(batch, seq, N_KV_HEADS, HEAD_DIM))
        vs.append(v_l.reshape(batch, seq, N_KV_HEADS, HEAD_DIM))
    y_prompt = xc.reshape(batch, seq, D_MODEL)
    y_sample = xl.reshape(dec_batch, dec_seq, D_MODEL)
    return (y_prompt, y_sample, jnp.stack(ks, axis=1), jnp.stack(vs, axis=1))
```

```python
import functools

import jax
import jax.numpy as jnp
from jax import lax
from jax.experimental import pallas as pl
from jax.experimental.pallas import tpu as pltpu

D_MODEL = 1024
DEPTH = 4
N_HEADS = 8
N_KV_HEADS = 2
HEAD_DIM = 64
GQA_GROUP = N_HEADS // N_KV_HEADS
ATTN_W = N_HEADS * HEAD_DIM
KV_W = N_KV_HEADS * HEAD_DIM
POOL_WINDOWS = (2, 4, 8, 16)
N_POOL_GROUPS = 4
POOL_W = D_MODEL // 2
POOL_GROUP_W = POOL_W // N_POOL_GROUPS
MIX_W = ATTN_W + POOL_W
IN_W = ATTN_W + 2 * KV_W + POOL_W
D_FF = 2816
GRID_W = 64
WINDOW = 128
ROPE_BASE = 10000.0
LN_EPS = 1e-5
DEEPNORM_ALPHA = (2 * DEPTH) ** 0.25
ATTN_SCALE = HEAD_DIM ** -0.5
NEG_INF = -1e30
LOG2E = 1.4426950408889634

LANES = 128
SUBLANES = 8
TILE = 1024
FF_CHUNK = 256
N_FF_CHUNKS = D_FF // FF_CHUNK
EPI_ROWS = 512
OUT_ROWS = 256
MOD_COLS = 1536
MOD_ROWS = 16
VMEM_LIMIT = 60000 * 1024

F32 = jnp.float32
BF16 = jnp.bfloat16


def _silu(x):
    return x / (1.0 + jnp.exp(-x))


def _layer_norm(y, g, b):
    mu = jnp.mean(y, axis=-1, keepdims=True)
    yc = y - mu
    var = jnp.mean(yc * yc, axis=-1, keepdims=True)
    return yc * lax.rsqrt(var + LN_EPS) * g + b


def _mod_slice(mod_ref, j):
    return mod_ref[:, j * D_MODEL:(j + 1) * D_MODEL]


COL_BLOCKS = D_MODEL // LANES


def _cols_load(ref, r0, n):
    return jnp.concatenate([ref[j, r0:r0 + n, :] for j in range(COL_BLOCKS)], axis=1)


def _cols_store(ref, r0, y):
    for j in range(COL_BLOCKS):
        ref[j, r0:r0 + y.shape[0], :] = y[:, j * LANES:(j + 1) * LANES]


def _store_permuted(ref, y, row0, seq):
    per = seq // SUBLANES
    for q in range(y.shape[0] // per):
        t0 = row0 + q * per
        s, a = t0 // seq, (t0 % seq) // per
        for j in range(COL_BLOCKS):
            ref[j, pl.ds(s * seq + a, per, stride=SUBLANES), :] = (
                y[q * per:(q + 1) * per, j * LANES:(j + 1) * LANES])


def _store_natural(o_ref, stage_ref, y, row0, seq):
    n = y.shape[0]
    per = seq // SUBLANES
    _cols_store(stage_ref, 0, y)
    span = min(n, seq)
    for q0 in range(0, n, span):
        s, b0 = (row0 + q0) // seq, ((row0 + q0) % seq) // SUBLANES
        for a in range(SUBLANES):
            t0 = s * seq + a * per + b0
            for j in range(COL_BLOCKS):
                o_ref[t0:t0 + span // SUBLANES, j * LANES:(j + 1) * LANES] = (
                    stage_ref[j, pl.ds(q0 + a, span // SUBLANES, stride=SUBLANES), :])


def _cols_spec():
    return pl.BlockSpec((COL_BLOCKS, TILE, LANES), lambda i: (0, i, 0))


def _layer_spec(shape, layer):
    idx = (layer,) + (0,) * len(shape)
    return pl.BlockSpec((None,) + tuple(shape), lambda i: idx, pipeline_mode=pl.Buffered(1))


def _mod_kernel(cond_ref, w_ref, b_ref, o_ref):
    a = _silu(cond_ref[...]).astype(BF16)
    o_ref[0] = jnp.dot(a, w_ref[0].astype(BF16), preferred_element_type=F32) + b_ref[0]


def _modulation(cond, w_mod, b_mod):
    n_col = (6 * D_MODEL) // MOD_COLS
    return pl.pallas_call(
        _mod_kernel,
        grid=(DEPTH, n_col),
        in_specs=[
            pl.BlockSpec((MOD_ROWS, D_MODEL), lambda l, j: (0, 0)),
            pl.BlockSpec((1, D_MODEL, MOD_COLS), lambda l, j: (l, 0, j)),
            pl.BlockSpec((1, 1, MOD_COLS), lambda l, j: (l, 0, j)),
        ],
        out_specs=pl.BlockSpec((1, MOD_ROWS, MOD_COLS), lambda l, j: (l, 0, j)),
        out_shape=jax.ShapeDtypeStruct((DEPTH, MOD_ROWS, 6 * D_MODEL), F32),
        compiler_params=pltpu.CompilerParams(
            dimension_semantics=("arbitrary", "arbitrary"), vmem_limit_bytes=VMEM_LIMIT),
        name="modulation",
    )(cond, w_mod, b_mod.reshape(DEPTH, 1, 6 * D_MODEL))


def _dup_halves(x):
    lo = lax.broadcasted_iota(jnp.int32, x.shape, 1) < HEAD_DIM
    xr = pltpu.roll(x, HEAD_DIM, axis=1)
    return jnp.where(lo, x, xr), jnp.where(lo, xr, x)


def _softmax_attend(lhs, sink_col, key_sets):
    def lane_blocks(a):
        return [a[:, j * LANES:(j + 1) * LANES] for j in range(a.shape[1] // LANES)]

    scores = []
    for k2, _, bias in key_sets:
        s = lax.dot_general(lhs, k2, (((1,), (1,)), ((), ())), preferred_element_type=F32)
        if bias is not None:
            rows, cols = s.shape
            s = (s.reshape(GQA_GROUP, rows // GQA_GROUP, cols) + bias[None]).reshape(rows, cols)
        scores.append(s)
    m_lanes = functools.reduce(jnp.maximum, [b for s in scores for b in lane_blocks(s)])
    m = jnp.maximum(jnp.max(m_lanes, axis=-1, keepdims=True), sink_col)
    e_lanes = None
    out = None
    for s, (_, v2, _) in zip(scores, key_sets):
        e = jnp.exp2(s - m)
        e_sum = functools.reduce(jnp.add, lane_blocks(e))
        e_lanes = e_sum if e_lanes is None else e_lanes + e_sum
        o = jnp.dot(e.astype(BF16), v2, preferred_element_type=F32)
        out = o if out is None else out + o
    denom = jnp.sum(e_lanes, axis=-1, keepdims=True) + jnp.exp2(sink_col - m)
    return out / denom


def _mixer_kernel(*refs, latent, seq, layer):
    if latent:
        (sink_ref, x_ref, mod_ref, w_in_ref, w_pool_ref, ps_ref, w_out_ref, g_ref, b_ref,
         kc_ref, vc_ref, rope_ref, x1_ref, q_s, k_s, v_s, mix_s, kc_s, vc_s) = refs
    else:
        (sink_ref, x_ref, mod_ref, w_in_ref, w_pool_ref, ps_ref, w_out_ref, g_ref, b_ref,
         x1_ref, k_out_ref, v_out_ref, q_s, k_s, v_s, mix_s) = refs

    h = (x_ref[...] * (1.0 + _mod_slice(mod_ref, 1)) + _mod_slice(mod_ref, 0)).astype(BF16)
    qkv = jnp.dot(h, w_in_ref[:, :ATTN_W + 2 * KV_W], preferred_element_type=F32)
    p = jnp.dot(h, w_in_ref[:, ATTN_W + 2 * KV_W:], preferred_element_type=F32)
    k = qkv[:, ATTN_W:ATTN_W + KV_W]
    v = qkv[:, ATTN_W + KV_W:ATTN_W + 2 * KV_W]

    n_seq = TILE // seq
    sub = lax.broadcasted_iota(jnp.int32, (SUBLANES, POOL_GROUP_W), 0)

    def edge_rows(a, first_fn, last_fn):
        pieces = []
        for s in range(n_seq):
            blk = a[s * seq:(s + 1) * seq]
            head = blk[:SUBLANES] if first_fn is None else first_fn(blk[:SUBLANES])
            tail = blk[seq - SUBLANES:] if last_fn is None else last_fn(blk[seq - SUBLANES:])
            pieces += [head, blk[SUBLANES:seq - SUBLANES], tail]
        return jnp.concatenate(pieces, axis=0)

    def shift(a, j):
        rolled = pltpu.roll(a, (-j) % TILE, axis=0)
        if j < 0:
            return edge_rows(rolled, lambda r: jnp.where(sub >= -j, r, 0.0), None)
        return edge_rows(rolled, None, lambda r: jnp.where(sub < SUBLANES - j, r, 0.0))

    def pool_group(g):
        pg = p[:, g * POOL_GROUP_W:(g + 1) * POOL_GROUP_W]
        half = POOL_WINDOWS[g] // 2
        left, right, n = shift(pg, -1), pg, 1
        while n < half:
            left = left + shift(left, -n)
            right = right + shift(right, n)
            n *= 2
        total = left + right
        first_cnt = jnp.minimum(sub + half, 2 * half).astype(F32)
        last_cnt = jnp.minimum(SUBLANES - sub + half, 2 * half).astype(F32)
        mean = edge_rows(total * (1.0 / (2 * half)),
                         lambda r: r * (2.0 * half) / first_cnt,
                         lambda r: r * (2.0 * half) / last_cnt)
        d = (mean - pg).astype(BF16)
        y = jnp.dot(d, w_pool_ref[g], preferred_element_type=F32)
        y = y * ps_ref[:, g * POOL_GROUP_W:(g + 1) * POOL_GROUP_W]
        mix_s[:, ATTN_W + g * POOL_GROUP_W:ATTN_W + (g + 1) * POOL_GROUP_W] = y.astype(BF16)

    if latent:
        cos = rope_ref[0]
        sin_up = rope_ref[1]
        sin_dn = rope_ref[2]

        def rope(t):
            return (t * cos + pltpu.roll(t, LANES - HEAD_DIM // 4, axis=1) * sin_up
                    + pltpu.roll(t, HEAD_DIM // 4, axis=1) * sin_dn)

        k = rope(k)
        pad = WINDOW
        zeros = jnp.zeros((pad, LANES), BF16)
        for hh in range(N_KV_HEADS):
            k_s[hh, 0:pad, :] = zeros
            k_s[hh, pad + TILE:2 * pad + TILE, :] = zeros
            v_s[hh, 0:pad, :] = zeros
            v_s[hh, pad + TILE:2 * pad + TILE, :] = zeros
        kc0, kc1 = _dup_halves(kc_ref[...])
        vc0, vc1 = _dup_halves(vc_ref[...])
        kc_s[0] = kc0.astype(BF16)
        kc_s[1] = kc1.astype(BF16)
        vc_s[0] = vc0.astype(BF16)
        vc_s[1] = vc1.astype(BF16)
    else:
        pad = 0
        k_out_ref[...] = k
        v_out_ref[...] = v

    for j in range(ATTN_W // LANES):
        qj = qkv[:, j * LANES:(j + 1) * LANES]
        if latent:
            qj = rope(qj)
        q_s[:, j * LANES:(j + 1) * LANES] = (qj * (ATTN_SCALE * LOG2E)).astype(BF16)
        pool_group(j)
    k0, k1 = _dup_halves(k)
    v0, v1 = _dup_halves(v)
    k_s[0, pad:pad + TILE, :] = k0.astype(BF16)
    k_s[1, pad:pad + TILE, :] = k1.astype(BF16)
    v_s[0, pad:pad + TILE, :] = v0.astype(BF16)
    v_s[1, pad:pad + TILE, :] = v1.astype(BF16)

    qb = WINDOW if latent else seq
    n_rows = GQA_GROUP * qb
    row = lax.broadcasted_iota(jnp.int32, (n_rows, 1), 0)
    lo_q = lax.broadcasted_iota(jnp.int32, (qb, LANES), 1) < HEAD_DIM
    if latent:
        r = lax.broadcasted_iota(jnp.int32, (qb, 3 * WINDOW), 0)
        c = lax.broadcasted_iota(jnp.int32, (qb, 3 * WINDOW), 1)
        in_band = jnp.abs(r - c + WINDOW) <= WINDOW

    def attend_block(i, carry):
        r0 = pl.multiple_of(i * qb, qb)
        if latent:
            kpos = (i - 1) * WINDOW + c
            valid = in_band & (kpos >= 0) & (kpos < TILE)
            bias = jnp.where(valid, 0.0, NEG_INF * LOG2E).astype(F32)
        for hh in range(N_KV_HEADS):
            parts = []
            for pair in range(GQA_GROUP // 2):
                c0 = hh * GQA_GROUP * HEAD_DIM + pair * LANES
                qp = q_s[pl.ds(r0, qb), c0:c0 + LANES]
                parts.append(jnp.where(lo_q, qp, jnp.zeros_like(qp)))
                parts.append(jnp.where(lo_q, jnp.zeros_like(qp), qp))
            lhs = jnp.concatenate(parts, axis=0)
            h0 = hh * GQA_GROUP
            sink_col = jnp.full((n_rows, 1), sink_ref[layer, h0 + GQA_GROUP - 1] * LOG2E, F32)
            for g in range(GQA_GROUP - 2, -1, -1):
                sink_col = jnp.where(row < (g + 1) * qb, sink_ref[layer, h0 + g] * LOG2E, sink_col)
            if latent:
                key_sets = [(kc_s[hh], vc_s[hh], None),
                            (k_s[hh, pl.ds(r0, 3 * WINDOW), :], v_s[hh, pl.ds(r0, 3 * WINDOW), :], bias)]
            else:
                key_sets = [(k_s[hh, pl.ds(r0, qb), :], v_s[hh, pl.ds(r0, qb), :], None)]
            o = _softmax_attend(lhs, sink_col, key_sets)
            for pair in range(GQA_GROUP // 2):
                c0 = hh * GQA_GROUP * HEAD_DIM + pair * LANES
                oa = o[(2 * pair) * qb:(2 * pair + 1) * qb]
                ob = o[(2 * pair + 1) * qb:(2 * pair + 2) * qb]
                mix_s[pl.ds(r0, qb), c0:c0 + LANES] = jnp.where(lo_q, oa, ob).astype(BF16)
        return carry

    lax.fori_loop(0, TILE // qb, attend_block, 0, unroll=2)

    for r0 in range(0, TILE, OUT_ROWS):
        mix = jnp.dot(mix_s[r0:r0 + OUT_ROWS, :], w_out_ref[...], preferred_element_type=F32)
        y = DEEPNORM_ALPHA * x_ref[r0:r0 + OUT_ROWS, :] + _mod_slice(mod_ref, 2) * mix
        _store_permuted(x1_ref, _layer_norm(y, g_ref[...], b_ref[...]), r0, seq)


def _mixer_call(x, mods, mod_row, sink, w_in, w_pool, pool_scale, w_out, ln_g, ln_b, *,
                layer, latent, seq, cache_k=None, cache_v=None, rope=None):
    n_tok = x.shape[0]
    n_tiles = n_tok // TILE
    in_specs = [
        pl.BlockSpec(memory_space=pltpu.SMEM),
        pl.BlockSpec((TILE, D_MODEL), lambda i: (i, 0)),
        pl.BlockSpec((None, None, 1, 6 * D_MODEL), lambda i: (layer, mod_row(i), 0, 0)),
        _layer_spec((D_MODEL, IN_W), layer),
        _layer_spec((N_POOL_GROUPS, POOL_GROUP_W, POOL_GROUP_W), layer),
        _layer_spec((1, POOL_W), layer),
        _layer_spec((MIX_W, D_MODEL), layer),
        _layer_spec((1, D_MODEL), layer),
        _layer_spec((1, D_MODEL), layer),
    ]
    args = [sink, x, mods, w_in, w_pool, pool_scale, w_out, ln_g, ln_b]
    pad = WINDOW if latent else 0
    scratch = [
        pltpu.VMEM((TILE, ATTN_W), BF16),
        pltpu.VMEM((N_KV_HEADS, TILE + 2 * pad, LANES), BF16),
        pltpu.VMEM((N_KV_HEADS, TILE + 2 * pad, LANES), BF16),
        pltpu.VMEM((TILE, MIX_W), BF16),
    ]
    x_out = jax.ShapeDtypeStruct((COL_BLOCKS, n_tok, LANES), F32)
    x_spec = _cols_spec()
    if latent:
        past = cache_k.shape[2]
        in_specs += [
            pl.BlockSpec((None, None, past, KV_W), lambda i: (i, layer, 0, 0)),
            pl.BlockSpec((None, None, past, KV_W), lambda i: (i, layer, 0, 0)),
            pl.BlockSpec((3, TILE, LANES), lambda i: (0, 0, 0), pipeline_mode=pl.Buffered(1)),
        ]
        args += [cache_k, cache_v, rope]
        scratch += [pltpu.VMEM((N_KV_HEADS, past, LANES), BF16),
                    pltpu.VMEM((N_KV_HEADS, past, LANES), BF16)]
        out_shape = x_out
        out_specs = x_spec
    else:
        kv_out = jax.ShapeDtypeStruct((n_tok, KV_W), F32)
        kv_spec = pl.BlockSpec((TILE, KV_W), lambda i: (i, 0))
        out_shape = (x_out, kv_out, kv_out)
        out_specs = (x_spec, kv_spec, kv_spec)
    return pl.pallas_call(
        functools.partial(_mixer_kernel, latent=latent, seq=seq, layer=layer),
        grid=(n_tiles,),
        in_specs=in_specs,
        out_specs=out_specs,
        out_shape=out_shape,
        scratch_shapes=scratch,
        compiler_params=pltpu.CompilerParams(
            dimension_semantics=("arbitrary",), vmem_limit_bytes=VMEM_LIMIT),
        name="mixer_latent" if latent else "mixer_context",
    )(*args)


def _ffn_kernel(x_ref, mod_ref, w_up_ref, cw_ref, cb_ref, w_down_ref, g_ref, b_ref, o_ref,
                h_s, u_s, act_s, stage_s, *, seq):
    n_seq = TILE // seq
    h_s[...] = (_cols_load(x_ref, 0, TILE) * (1.0 + _mod_slice(mod_ref, 4))
                + _mod_slice(mod_ref, 3)).astype(BF16)
    sub = lax.broadcasted_iota(jnp.int32, (SUBLANES, FF_CHUNK), 0)

    def up_project(c):
        hb = h_s[...]
        for part in range(2):
            col = part * D_FF + c * FF_CHUNK
            u_s[c % 2, part] = jnp.dot(hb, w_up_ref[:, col:col + FF_CHUNK],
                                       preferred_element_type=F32)

    def conv_swiglu(c):
        for s in range(n_seq):
            halves = []
            for part in range(2):
                col = part * D_FF + c * FF_CHUNK
                cw = cw_ref[:, col:col + FF_CHUNK]
                cur = u_s[c % 2, part, s * seq:(s + 1) * seq, :]
                wrap_prev = jnp.where(sub == 0, 0.0, pltpu.roll(cur[seq - SUBLANES:], 1, axis=0))
                wrap_next = jnp.where(sub == SUBLANES - 1, 0.0,
                                      pltpu.roll(cur[:SUBLANES], SUBLANES - 1, axis=0))
                prev = jnp.concatenate([wrap_prev, cur[:seq - SUBLANES]], axis=0)
                nxt = jnp.concatenate([cur[SUBLANES:], wrap_next], axis=0)
                halves.append(prev * cw[0:1] + cur * cw[1:2] + nxt * cw[2:3]
                              + cb_ref[:, col:col + FF_CHUNK])
            a, g = halves
            act_s[s * seq:(s + 1) * seq, c * FF_CHUNK:(c + 1) * FF_CHUNK] = (_silu(g) * a).astype(BF16)

    up_project(0)
    for c in range(N_FF_CHUNKS):
        if c + 1 < N_FF_CHUNKS:
            up_project(c + 1)
        conv_swiglu(c)

    for r0 in range(0, TILE, EPI_ROWS):
        ff = jnp.dot(act_s[r0:r0 + EPI_ROWS, :], w_down_ref[...], preferred_element_type=F32)
        y = DEEPNORM_ALPHA * _cols_load(x_ref, r0, EPI_ROWS) + _mod_slice(mod_ref, 5) * ff
        _store_natural(o_ref, stage_s, _layer_norm(y, g_ref[...], b_ref[...]), r0, seq)


def _ffn_call(x, mods, mod_row, w_up, conv_w, conv_b, w_down, ln_g, ln_b, *, layer, seq, name):
    n_tok = x.shape[1]
    scratch = [pltpu.VMEM((TILE, D_MODEL), BF16),
               pltpu.VMEM((2, 2, TILE, FF_CHUNK), F32),
               pltpu.VMEM((TILE, D_FF), BF16),
               pltpu.VMEM((COL_BLOCKS, EPI_ROWS, LANES), F32)]
    out_specs = pl.BlockSpec((TILE, D_MODEL), lambda i: (i, 0))
    out_shape = jax.ShapeDtypeStruct((n_tok, D_MODEL), F32)
    return pl.pallas_call(
        functools.partial(_ffn_kernel, seq=seq),
        grid=(n_tok // TILE,),
        in_specs=[
            _cols_spec(),
            pl.BlockSpec((None, None, 1, 6 * D_MODEL), lambda i: (layer, mod_row(i), 0, 0)),
            _layer_spec((D_MODEL, 2 * D_FF), layer),
            _layer_spec((3, 2 * D_FF), layer),
            _layer_spec((1, 2 * D_FF), layer),
            _layer_spec((D_FF, D_MODEL), layer),
            _layer_spec((1, D_MODEL), layer),
            _layer_spec((1, D_MODEL), layer),
        ],
        out_specs=out_specs,
        out_shape=out_shape,
        scratch_shapes=scratch,
        compiler_params=pltpu.CompilerParams(
            dimension_semantics=("arbitrary",), vmem_limit_bytes=VMEM_LIMIT),
        name=name,
    )(x, mods, w_up, conv_w, conv_b, w_down, ln_g, ln_b)


def _rope_tables(n_pos):
    half = HEAD_DIM // 2
    t = jnp.arange(n_pos)
    inv_freq = ROPE_BASE ** (-jnp.arange(0, half, 2, dtype=F32) / half)

    def ang(p):
        a = p.astype(F32)[:, None] * inv_freq[None, :]
        return jnp.concatenate([a, a], axis=-1)

    a = jnp.concatenate([ang(t // GRID_W), ang(t % GRID_W)], axis=-1)
    cos = jnp.cos(a)
    sin = jnp.sin(a)
    first = (jnp.arange(HEAD_DIM) % half) < half // 2
    sin_up = jnp.where(first[None, :], -sin, 0.0)
    sin_dn = jnp.where(first[None, :], 0.0, sin)
    tabs = jnp.stack([cos, sin_up, sin_dn])
    return jnp.concatenate([tabs, tabs], axis=-1).astype(F32)


def kernel(x_prompt, x_sample, cache_k, cache_v, c, c_ctx, w_mod, b_mod, w_in, attn_sink,
           w_pool, pool_scale, w_out, ln1_g, ln1_b, w_up, conv_w, conv_b, w_down, ln2_g, ln2_b):
    batch, seq, _ = x_prompt.shape
    dec_batch, dec_seq, _ = x_sample.shape
    past = cache_k.shape[2]
    assert TILE % seq == 0 and dec_seq == TILE and seq & (seq - 1) == 0
    assert (batch * seq) % TILE == 0 and past % 16 == 0

    cond = jnp.zeros((MOD_ROWS, D_MODEL), F32).at[0].set(c_ctx).at[1:1 + dec_batch].set(c)
    mods = _modulation(cond, w_mod, b_mod).reshape(DEPTH, MOD_ROWS, 1, 6 * D_MODEL)

    mixer_w = (attn_sink, w_in.astype(BF16), w_pool.astype(BF16),
               pool_scale.reshape(DEPTH, 1, POOL_W), w_out.astype(BF16),
               ln1_g.reshape(DEPTH, 1, D_MODEL), ln1_b.reshape(DEPTH, 1, D_MODEL))
    ffn_w = (w_up.astype(BF16), conv_w, conv_b.reshape(DEPTH, 1, 2 * D_FF), w_down.astype(BF16),
             ln2_g.reshape(DEPTH, 1, D_MODEL), ln2_b.reshape(DEPTH, 1, D_MODEL))
    rope = _rope_tables(dec_seq)
    cache_k_r = cache_k.reshape(dec_batch, DEPTH, past, KV_W)
    cache_v_r = cache_v.reshape(dec_batch, DEPTH, past, KV_W)

    xc = x_prompt.reshape(batch * seq, D_MODEL)
    xl = x_sample.reshape(dec_batch * dec_seq, D_MODEL)
    ctx_row = lambda i: 0
    lat_row = lambda i: i + 1
    ks, vs = [], []
    for l in range(DEPTH):
        xc, k_l, v_l = _mixer_call(xc, mods, ctx_row, *mixer_w, layer=l, latent=False, seq=seq)
        xc = _ffn_call(xc, mods, ctx_row, *ffn_w, layer=l, seq=seq, name="ffn_context")
        xl = _mixer_call(xl, mods, lat_row, *mixer_w, layer=l, latent=True, seq=dec_seq,
                         cache_k=cache_k_r, cache_v=cache_v_r, rope=rope)
        xl = _ffn_call(xl, mods, lat_row, *ffn_w, layer=l, seq=dec_seq, name="ffn_latent")
        ks.append(k_l.reshape(batch, seq, N_KV_HEADS, HEAD_DIM))
        vs.append(v_l.reshape(batch, seq, N_KV_HEADS, HEAD_DIM))
    y_prompt = xc.reshape(batch, seq, D_MODEL)
    y_sample = xl.reshape(dec_batch, dec_seq, D_MODEL)
    return (y_prompt, y_sample, jnp.stack(ks, axis=1), jnp.stack(vs, axis=1))
```

```python
import functools

import jax
import jax.numpy as jnp
from jax import lax
from jax.experimental import pallas as pl
from jax.experimental.pallas import tpu as pltpu

D_MODEL = 1024
DEPTH = 4
N_HEADS = 8
N_KV_HEADS = 2
HEAD_DIM = 64
GQA_GROUP = N_HEADS // N_KV_HEADS
ATTN_W = N_HEADS * HEAD_DIM
KV_W = N_KV_HEADS * HEAD_DIM
POOL_WINDOWS = (2, 4, 8, 16)
N_POOL_GROUPS = 4
POOL_W = D_MODEL // 2
POOL_GROUP_W = POOL_W // N_POOL_GROUPS
MIX_W = ATTN_W + POOL_W
IN_W = ATTN_W + 2 * KV_W + POOL_W
D_FF = 2816
GRID_W = 64
WINDOW = 128
ROPE_BASE = 10000.0
LN_EPS = 1e-5
DEEPNORM_ALPHA = (2 * DEPTH) ** 0.25
ATTN_SCALE = HEAD_DIM ** -0.5
NEG_INF = -1e30
LOG2E = 1.4426950408889634

LANES = 128
SUBLANES = 8
TILE = 1024
FF_CHUNK = 256
N_FF_CHUNKS = D_FF // FF_CHUNK
EPI_ROWS = 512
OUT_ROWS = 256
ATTN_UNROLL = 2
MOD_COLS = 1536
MOD_ROWS = 16
VMEM_LIMIT = 60000 * 1024

F32 = jnp.float32
BF16 = jnp.bfloat16


def _silu(x):
    return x / (1.0 + jnp.exp(-x))


def _layer_norm(y, g, b):
    mu = jnp.mean(y, axis=-1, keepdims=True)
    yc = y - mu
    var = jnp.mean(yc * yc, axis=-1, keepdims=True)
    return yc * lax.rsqrt(var + LN_EPS) * g + b


def _mod_slice(mod_ref, j):
    return mod_ref[:, j * D_MODEL:(j + 1) * D_MODEL]


def _mm(a, w):
    return lax.dot_general(a, w, (((1,), (0,)), ((), ())), preferred_element_type=F32)


COL_BLOCKS = D_MODEL // LANES


def _cols_load(ref, r0, n):
    return jnp.concatenate([ref[j, r0:r0 + n, :] for j in range(COL_BLOCKS)], axis=1)


def _cols_store(ref, r0, y):
    for j in range(COL_BLOCKS):
        ref[j, r0:r0 + y.shape[0], :] = y[:, j * LANES:(j + 1) * LANES]


def _store_permuted(ref, y, row0, seq):
    per = seq // SUBLANES
    for q in range(y.shape[0] // per):
        t0 = row0 + q * per
        s, a = t0 // seq, (t0 % seq) // per
        for j in range(COL_BLOCKS):
            ref[j, pl.ds(s * seq + a, per, stride=SUBLANES), :] = (
                y[q * per:(q + 1) * per, j * LANES:(j + 1) * LANES])


def _store_natural(o_ref, stage_ref, y, row0, seq):
    n = y.shape[0]
    per = seq // SUBLANES
    _cols_store(stage_ref, 0, y)
    span = min(n, seq)
    for q0 in range(0, n, span):
        s, b0 = (row0 + q0) // seq, ((row0 + q0) % seq) // SUBLANES
        for a in range(SUBLANES):
            t0 = s * seq + a * per + b0
            for j in range(COL_BLOCKS):
                o_ref[t0:t0 + span // SUBLANES, j * LANES:(j + 1) * LANES] = (
                    stage_ref[j, pl.ds(q0 + a, span // SUBLANES, stride=SUBLANES), :])


def _cols_spec(tile_of=lambda i: i):
    return pl.BlockSpec((COL_BLOCKS, TILE, LANES), lambda i: (0, tile_of(i), 0))


def _layer_spec(shape, layer):
    idx = (layer,) + (0,) * len(shape)
    return pl.BlockSpec((None,) + tuple(shape), lambda i: idx, pipeline_mode=pl.Buffered(1))


def _mod_kernel(cond_ref, w_ref, b_ref, o_ref):
    a = _silu(cond_ref[...]).astype(BF16)
    o_ref[0] = jnp.dot(a, w_ref[0].astype(BF16), preferred_element_type=F32) + b_ref[0]


def _modulation(cond, w_mod, b_mod):
    n_col = (6 * D_MODEL) // MOD_COLS
    return pl.pallas_call(
        _mod_kernel,
        grid=(DEPTH, n_col),
        in_specs=[
            pl.BlockSpec((MOD_ROWS, D_MODEL), lambda l, j: (0, 0)),
            pl.BlockSpec((1, D_MODEL, MOD_COLS), lambda l, j: (l, 0, j)),
            pl.BlockSpec((1, 1, MOD_COLS), lambda l, j: (l, 0, j)),
        ],
        out_specs=pl.BlockSpec((1, MOD_ROWS, MOD_COLS), lambda l, j: (l, 0, j)),
        out_shape=jax.ShapeDtypeStruct((DEPTH, MOD_ROWS, 6 * D_MODEL), F32),
        compiler_params=pltpu.CompilerParams(
            dimension_semantics=("arbitrary", "arbitrary"), vmem_limit_bytes=VMEM_LIMIT),
        name="modulation",
    )(cond, w_mod, b_mod.reshape(DEPTH, 1, 6 * D_MODEL))


def _dup_halves(x):
    lo = lax.broadcasted_iota(jnp.int32, x.shape, 1) < HEAD_DIM
    xr = pltpu.roll(x, HEAD_DIM, axis=1)
    return jnp.where(lo, x, xr), jnp.where(lo, xr, x)


def _softmax_attend(lhs, sink_col, key_sets):
    def lane_blocks(a):
        return [a[:, j * LANES:(j + 1) * LANES] for j in range(a.shape[1] // LANES)]

    scores = []
    for k2, _, bias in key_sets:
        s = lax.dot_general(lhs, k2, (((1,), (1,)), ((), ())), preferred_element_type=F32)
        if bias is not None:
            rows, cols = s.shape
            s = (s.reshape(GQA_GROUP, rows // GQA_GROUP, cols) + bias[None]).reshape(rows, cols)
        scores.append(s)
    m_lanes = functools.reduce(jnp.maximum, [b for s in scores for b in lane_blocks(s)])
    m = jnp.maximum(jnp.max(m_lanes, axis=-1, keepdims=True), sink_col)
    e_lanes = None
    out = None
    for s, (_, v2, _) in zip(scores, key_sets):
        e = jnp.exp2(s - m)
        e_sum = functools.reduce(jnp.add, lane_blocks(e))
        e_lanes = e_sum if e_lanes is None else e_lanes + e_sum
        o = jnp.dot(e.astype(BF16), v2, preferred_element_type=F32)
        out = o if out is None else out + o
    denom = jnp.sum(e_lanes, axis=-1, keepdims=True) + jnp.exp2(sink_col - m)
    return out / denom


def _mixer_kernel(*refs, latent, seq, layer):
    if latent:
        (sink_ref, x_ref, mod_ref, w_in_ref, w_pool_ref, ps_ref, w_out_ref, g_ref, b_ref,
         kc_ref, vc_ref, rope_ref, x1_ref, q_s, k_s, v_s, mix_s, kc_s, vc_s) = refs
    else:
        (sink_ref, x_ref, mod_ref, w_in_ref, w_pool_ref, ps_ref, w_out_ref, g_ref, b_ref,
         x1_ref, k_out_ref, v_out_ref, q_s, k_s, v_s, mix_s) = refs

    h = (x_ref[...] * (1.0 + _mod_slice(mod_ref, 1)) + _mod_slice(mod_ref, 0)).astype(BF16)
    qkv = _mm(h, w_in_ref[:, :ATTN_W + 2 * KV_W])
    p = _mm(h, w_in_ref[:, ATTN_W + 2 * KV_W:])
    k = qkv[:, ATTN_W:ATTN_W + KV_W]
    v = qkv[:, ATTN_W + KV_W:ATTN_W + 2 * KV_W]

    n_seq = TILE // seq
    sub = lax.broadcasted_iota(jnp.int32, (SUBLANES, POOL_GROUP_W), 0)

    def edge_rows(a, first_fn, last_fn):
        pieces = []
        for s in range(n_seq):
            blk = a[s * seq:(s + 1) * seq]
            head = blk[:SUBLANES] if first_fn is None else first_fn(blk[:SUBLANES])
            tail = blk[seq - SUBLANES:] if last_fn is None else last_fn(blk[seq - SUBLANES:])
            pieces += [head, blk[SUBLANES:seq - SUBLANES], tail]
        return jnp.concatenate(pieces, axis=0)

    def shift(a, j):
        rolled = pltpu.roll(a, (-j) % TILE, axis=0)
        if j < 0:
            return edge_rows(rolled, lambda r: jnp.where(sub >= -j, r, 0.0), None)
        return edge_rows(rolled, None, lambda r: jnp.where(sub < SUBLANES - j, r, 0.0))

    def pool_group(g):
        pg = p[:, g * POOL_GROUP_W:(g + 1) * POOL_GROUP_W]
        half = POOL_WINDOWS[g] // 2
        left, right, n = shift(pg, -1), pg, 1
        while n < half:
            left = left + shift(left, -n)
            right = right + shift(right, n)
            n *= 2
        total = left + right
        first_cnt = jnp.minimum(sub + half, 2 * half).astype(F32)
        last_cnt = jnp.minimum(SUBLANES - sub + half, 2 * half).astype(F32)
        mean = edge_rows(total * (1.0 / (2 * half)),
                         lambda r: r * (2.0 * half) / first_cnt,
                         lambda r: r * (2.0 * half) / last_cnt)
        d = (mean - pg).astype(BF16)
        y = _mm(d, w_pool_ref[g])
        y = y * ps_ref[:, g * POOL_GROUP_W:(g + 1) * POOL_GROUP_W]
        mix_s[:, ATTN_W + g * POOL_GROUP_W:ATTN_W + (g + 1) * POOL_GROUP_W] = y.astype(BF16)

    if latent:
        cos = rope_ref[0]
        sin_up = rope_ref[1]
        sin_dn = rope_ref[2]

        def rope(t):
            return (t * cos + pltpu.roll(t, LANES - HEAD_DIM // 4, axis=1) * sin_up
                    + pltpu.roll(t, HEAD_DIM // 4, axis=1) * sin_dn)

        k = rope(k)
        pad = WINDOW
        zeros = jnp.zeros((pad, LANES), BF16)
        for hh in range(N_KV_HEADS):
            k_s[hh, 0:pad, :] = zeros
            k_s[hh, pad + TILE:2 * pad + TILE, :] = zeros
            v_s[hh, 0:pad, :] = zeros
            v_s[hh, pad + TILE:2 * pad + TILE, :] = zeros
        kc0, kc1 = _dup_halves(kc_ref[...])
        vc0, vc1 = _dup_halves(vc_ref[...])
        kc_s[0] = kc0.astype(BF16)
        kc_s[1] = kc1.astype(BF16)
        vc_s[0] = vc0.astype(BF16)
        vc_s[1] = vc1.astype(BF16)
    else:
        pad = 0
        k_out_ref[...] = k
        v_out_ref[...] = v

    for j in range(ATTN_W // LANES):
        qj = qkv[:, j * LANES:(j + 1) * LANES]
        if latent:
            qj = rope(qj)
        q_s[:, j * LANES:(j + 1) * LANES] = (qj * (ATTN_SCALE * LOG2E)).astype(BF16)
        pool_group(j)
    k0, k1 = _dup_halves(k)
    v0, v1 = _dup_halves(v)
    k_s[0, pad:pad + TILE, :] = k0.astype(BF16)
    k_s[1, pad:pad + TILE, :] = k1.astype(BF16)
    v_s[0, pad:pad + TILE, :] = v0.astype(BF16)
    v_s[1, pad:pad + TILE, :] = v1.astype(BF16)

    qb = WINDOW if latent else seq
    n_rows = GQA_GROUP * qb
    row = lax.broadcasted_iota(jnp.int32, (n_rows, 1), 0)
    lo_q = lax.broadcasted_iota(jnp.int32, (qb, LANES), 1) < HEAD_DIM
    if latent:
        r = lax.broadcasted_iota(jnp.int32, (qb, 3 * WINDOW), 0)
        c = lax.broadcasted_iota(jnp.int32, (qb, 3 * WINDOW), 1)
        in_band = jnp.abs(r - c + WINDOW) <= WINDOW

    def attend_block(i, carry):
        r0 = pl.multiple_of(i * qb, qb)
        if latent:
            kpos = (i - 1) * WINDOW + c
            valid = in_band & (kpos >= 0) & (kpos < TILE)
            bias = jnp.where(valid, 0.0, NEG_INF * LOG2E).astype(F32)
        for hh in range(N_KV_HEADS):
            parts = []
            for pair in range(GQA_GROUP // 2):
                c0 = hh * GQA_GROUP * HEAD_DIM + pair * LANES
                qp = q_s[pl.ds(r0, qb), c0:c0 + LANES]
                parts.append(jnp.where(lo_q, qp, jnp.zeros_like(qp)))
                parts.append(jnp.where(lo_q, jnp.zeros_like(qp), qp))
            lhs = jnp.concatenate(parts, axis=0)
            h0 = hh * GQA_GROUP
            sink_col = jnp.full((n_rows, 1), sink_ref[layer, h0 + GQA_GROUP - 1] * LOG2E, F32)
            for g in range(GQA_GROUP - 2, -1, -1):
                sink_col = jnp.where(row < (g + 1) * qb, sink_ref[layer, h0 + g] * LOG2E, sink_col)
            if latent:
                key_sets = [(kc_s[hh], vc_s[hh], None),
                            (k_s[hh, pl.ds(r0, 3 * WINDOW), :], v_s[hh, pl.ds(r0, 3 * WINDOW), :], bias)]
            else:
                key_sets = [(k_s[hh, pl.ds(r0, qb), :], v_s[hh, pl.ds(r0, qb), :], None)]
            o = _softmax_attend(lhs, sink_col, key_sets)
            for pair in range(GQA_GROUP // 2):
                c0 = hh * GQA_GROUP * HEAD_DIM + pair * LANES
                oa = o[(2 * pair) * qb:(2 * pair + 1) * qb]
                ob = o[(2 * pair + 1) * qb:(2 * pair + 2) * qb]
                mix_s[pl.ds(r0, qb), c0:c0 + LANES] = jnp.where(lo_q, oa, ob).astype(BF16)
        return carry

    lax.fori_loop(0, TILE // qb, attend_block, 0, unroll=ATTN_UNROLL)

    for r0 in range(0, TILE, OUT_ROWS):
        mix = _mm(mix_s[r0:r0 + OUT_ROWS, :], w_out_ref[...])
        y = DEEPNORM_ALPHA * x_ref[r0:r0 + OUT_ROWS, :] + _mod_slice(mod_ref, 2) * mix
        _store_permuted(x1_ref, _layer_norm(y, g_ref[...], b_ref[...]), r0, seq)


def _mixer_call(x, mods, mod_row, sink, w_in, w_pool, pool_scale, w_out, ln_g, ln_b, *,
                layer, latent, seq, n_tiles, in_tile0, cache_k=None, cache_v=None, rope=None):
    n_tok = n_tiles * TILE
    in_specs = [
        pl.BlockSpec(memory_space=pltpu.SMEM),
        pl.BlockSpec((TILE, D_MODEL), lambda i: (i + in_tile0, 0)),
        pl.BlockSpec((None, None, 1, 6 * D_MODEL), lambda i: (layer, mod_row(i), 0, 0)),
        _layer_spec((D_MODEL, IN_W), layer),
        _layer_spec((N_POOL_GROUPS, POOL_GROUP_W, POOL_GROUP_W), layer),
        _layer_spec((1, POOL_W), layer),
        _layer_spec((MIX_W, D_MODEL), layer),
        _layer_spec((1, D_MODEL), layer),
        _layer_spec((1, D_MODEL), layer),
    ]
    args = [sink, x, mods, w_in, w_pool, pool_scale, w_out, ln_g, ln_b]
    pad = WINDOW if latent else 0
    scratch = [
        pltpu.VMEM((TILE, ATTN_W), BF16),
        pltpu.VMEM((N_KV_HEADS, TILE + 2 * pad, LANES), BF16),
        pltpu.VMEM((N_KV_HEADS, TILE + 2 * pad, LANES), BF16),
        pltpu.VMEM((TILE, MIX_W), BF16),
    ]
    x_out = jax.ShapeDtypeStruct((COL_BLOCKS, n_tok, LANES), F32)
    x_spec = _cols_spec()
    if latent:
        past = cache_k.shape[2]
        in_specs += [
            pl.BlockSpec((None, None, past, KV_W), lambda i: (i, layer, 0, 0)),
            pl.BlockSpec((None, None, past, KV_W), lambda i: (i, layer, 0, 0)),
            pl.BlockSpec((3, TILE, LANES), lambda i: (0, 0, 0), pipeline_mode=pl.Buffered(1)),
        ]
        args += [cache_k, cache_v, rope]
        scratch += [pltpu.VMEM((N_KV_HEADS, past, LANES), BF16),
                    pltpu.VMEM((N_KV_HEADS, past, LANES), BF16)]
        out_shape = x_out
        out_specs = x_spec
    else:
        kv_out = jax.ShapeDtypeStruct((n_tok, KV_W), F32)
        kv_spec = pl.BlockSpec((TILE, KV_W), lambda i: (i, 0))
        out_shape = (x_out, kv_out, kv_out)
        out_specs = (x_spec, kv_spec, kv_spec)
    return pl.pallas_call(
        functools.partial(_mixer_kernel, latent=latent, seq=seq, layer=layer),
        grid=(n_tiles,),
        in_specs=in_specs,
        out_specs=out_specs,
        out_shape=out_shape,
        scratch_shapes=scratch,
        compiler_params=pltpu.CompilerParams(
            dimension_semantics=("arbitrary",), vmem_limit_bytes=VMEM_LIMIT),
        name="mixer_latent" if latent else "mixer_context",
    )(*args)


def _ffn_kernel(*refs, groups):
    x_refs, rest = refs[:len(groups)], refs[len(groups):]
    step = pl.program_id(0)
    tile0 = 0
    for x_ref, (n_tiles, seq) in zip(x_refs, groups):
        tile_fn = functools.partial(_ffn_tile, x_ref, *rest, seq=seq)
        if len(groups) == 1:
            tile_fn()
        else:
            pl.when((step >= tile0) & (step < tile0 + n_tiles))(tile_fn)
        tile0 += n_tiles


def _ffn_tile(x_ref, mod_ref, w_up_ref, cw_ref, cb_ref, w_down_ref, g_ref, b_ref, o_ref,
              h_s, u_s, act_s, stage_s, *, seq):
    n_seq = TILE // seq
    h_s[...] = (_cols_load(x_ref, 0, TILE) * (1.0 + _mod_slice(mod_ref, 4))
                + _mod_slice(mod_ref, 3)).astype(BF16)
    sub = lax.broadcasted_iota(jnp.int32, (SUBLANES, FF_CHUNK), 0)

    def up_project(c):
        hb = h_s[...]
        for part in range(2):
            col = part * D_FF + c * FF_CHUNK
            u_s[c % 2, part] = jnp.dot(hb, w_up_ref[:, col:col + FF_CHUNK],
                                       preferred_element_type=F32)

    def conv_swiglu(c):
        for s in range(n_seq):
            halves = []
            for part in range(2):
                col = part * D_FF + c * FF_CHUNK
                cw = cw_ref[:, col:col + FF_CHUNK]
                cur = u_s[c % 2, part, s * seq:(s + 1) * seq, :]
                wrap_prev = jnp.where(sub == 0, 0.0, pltpu.roll(cur[seq - SUBLANES:], 1, axis=0))
                wrap_next = jnp.where(sub == SUBLANES - 1, 0.0,
                                      pltpu.roll(cur[:SUBLANES], SUBLANES - 1, axis=0))
                prev = jnp.concatenate([wrap_prev, cur[:seq - SUBLANES]], axis=0)
                nxt = jnp.concatenate([cur[SUBLANES:], wrap_next], axis=0)
                halves.append(prev * cw[0:1] + cur * cw[1:2] + nxt * cw[2:3]
                              + cb_ref[:, col:col + FF_CHUNK])
            a, g = halves
            act_s[s * seq:(s + 1) * seq, c * FF_CHUNK:(c + 1) * FF_CHUNK] = (_silu(g) * a).astype(BF16)

    up_project(0)
    for c in range(N_FF_CHUNKS):
        if c + 1 < N_FF_CHUNKS:
            up_project(c + 1)
        conv_swiglu(c)

    for r0 in range(0, TILE, EPI_ROWS):
        ff = _mm(act_s[r0:r0 + EPI_ROWS, :], w_down_ref[...])
        y = DEEPNORM_ALPHA * _cols_load(x_ref, r0, EPI_ROWS) + _mod_slice(mod_ref, 5) * ff
        _store_natural(o_ref, stage_s, _layer_norm(y, g_ref[...], b_ref[...]), r0, seq)


def _ffn_call(xs, groups, mods, mod_row, w_up, conv_w, conv_b, w_down, ln_g, ln_b, *, layer,
              name):
    n_steps = sum(n for n, _ in groups)
    scratch = [pltpu.VMEM((TILE, D_MODEL), BF16),
               pltpu.VMEM((2, 2, TILE, FF_CHUNK), F32),
               pltpu.VMEM((TILE, D_FF), BF16),
               pltpu.VMEM((COL_BLOCKS, EPI_ROWS, LANES), F32)]
    out_specs = pl.BlockSpec((TILE, D_MODEL), lambda i: (i, 0))
    out_shape = jax.ShapeDtypeStruct((n_steps * TILE, D_MODEL), F32)
    x_specs, tile0 = [], 0
    for n_tiles, _ in groups:
        x_specs.append(_cols_spec(
            lambda i, tile0=tile0, n_tiles=n_tiles: jnp.clip(i - tile0, 0, n_tiles - 1)))
        tile0 += n_tiles
    return pl.pallas_call(
        functools.partial(_ffn_kernel, groups=tuple(groups)),
        grid=(n_steps,),
        in_specs=x_specs + [
            pl.BlockSpec((None, None, 1, 6 * D_MODEL), lambda i: (layer, mod_row(i), 0, 0)),
            _layer_spec((D_MODEL, 2 * D_FF), layer),
            _layer_spec((3, 2 * D_FF), layer),
            _layer_spec((1, 2 * D_FF), layer),
            _layer_spec((D_FF, D_MODEL), layer),
            _layer_spec((1, D_MODEL), layer),
            _layer_spec((1, D_MODEL), layer),
        ],
        out_specs=out_specs,
        out_shape=out_shape,
        scratch_shapes=scratch,
        compiler_params=pltpu.CompilerParams(
            dimension_semantics=("arbitrary",), vmem_limit_bytes=VMEM_LIMIT),
        name=name,
    )(*xs, mods, w_up, conv_w, conv_b, w_down, ln_g, ln_b)


def _rope_tables(n_pos):
    half = HEAD_DIM // 2
    t = jnp.arange(n_pos)
    inv_freq = ROPE_BASE ** (-jnp.arange(0, half, 2, dtype=F32) / half)

    def ang(p):
        a = p.astype(F32)[:, None] * inv_freq[None, :]
        return jnp.concatenate([a, a], axis=-1)

    a = jnp.concatenate([ang(t // GRID_W), ang(t % GRID_W)], axis=-1)
    cos = jnp.cos(a)
    sin = jnp.sin(a)
    first = (jnp.arange(HEAD_DIM) % half) < half // 2
    sin_up = jnp.where(first[None, :], -sin, 0.0)
    sin_dn = jnp.where(first[None, :], 0.0, sin)
    tabs = jnp.stack([cos, sin_up, sin_dn])
    return jnp.concatenate([tabs, tabs], axis=-1).astype(F32)


def kernel(x_prompt, x_sample, cache_k, cache_v, c, c_ctx, w_mod, b_mod, w_in, attn_sink,
           w_pool, pool_scale, w_out, ln1_g, ln1_b, w_up, conv_w, conv_b, w_down, ln2_g, ln2_b):
    batch, seq, _ = x_prompt.shape
    dec_batch, dec_seq, _ = x_sample.shape
    past = cache_k.shape[2]
    assert TILE % seq == 0 and dec_seq == TILE and seq & (seq - 1) == 0
    assert (batch * seq) % TILE == 0 and past % 16 == 0

    cond = jnp.zeros((MOD_ROWS, D_MODEL), F32).at[0].set(c_ctx).at[1:1 + dec_batch].set(c)
    mods = _modulation(cond, w_mod, b_mod).reshape(DEPTH, MOD_ROWS, 1, 6 * D_MODEL)

    mixer_w = (attn_sink, w_in, w_pool,
               pool_scale.reshape(DEPTH, 1, POOL_W), w_out,
               ln1_g.reshape(DEPTH, 1, D_MODEL), ln1_b.reshape(DEPTH, 1, D_MODEL))
    ffn_w = (w_up.astype(BF16), conv_w, conv_b.reshape(DEPTH, 1, 2 * D_FF), w_down,
             ln2_g.reshape(DEPTH, 1, D_MODEL), ln2_b.reshape(DEPTH, 1, D_MODEL))
    rope = _rope_tables(dec_seq)
    cache_k_r = cache_k.reshape(dec_batch, DEPTH, past, KV_W)
    cache_v_r = cache_v.reshape(dec_batch, DEPTH, past, KV_W)

    xc = x_prompt.reshape(batch * seq, D_MODEL)
    xl = x_sample.reshape(dec_batch * dec_seq, D_MODEL)
    ctx_tiles = batch * seq // TILE
    lat_tiles = dec_batch * dec_seq // TILE
    ctx_row = lambda i: 0
    lat_row = lambda i: i + 1
    ctx_group, lat_group = (ctx_tiles, seq), (lat_tiles, dec_seq)
    ks, vs = [], []
    for l in range(DEPTH):
        xc, k_l, v_l = _mixer_call(xc, mods, ctx_row, *mixer_w, layer=l, latent=False, seq=seq,
                                   n_tiles=ctx_tiles, in_tile0=0)
        xl = _mixer_call(xl, mods, lat_row, *mixer_w, layer=l, latent=True, seq=dec_seq,
                         n_tiles=lat_tiles, in_tile0=0,
                         cache_k=cache_k_r, cache_v=cache_v_r, rope=rope)
        xc = _ffn_call([xc], [ctx_group], mods, ctx_row, *ffn_w, layer=l, name="ffn_context")
        xl = _ffn_call([xl], [lat_group], mods, lat_row, *ffn_w, layer=l, name="ffn_latent")
        ks.append(k_l.reshape(batch, seq, KV_W))
        vs.append(v_l.reshape(batch, seq, KV_W))
    y_prompt = xc.reshape(batch, seq, D_MODEL)
    y_sample = xl.reshape(dec_batch, dec_seq, D_MODEL)
    cache_shape = (batch, DEPTH, seq, N_KV_HEADS, HEAD_DIM)
    return (y_prompt, y_sample, jnp.stack(ks, axis=1).reshape(cache_shape),
            jnp.stack(vs, axis=1).reshape(cache_shape))
```

```python
import functools

import jax
import jax.numpy as jnp
from jax import lax
from jax.experimental import pallas as pl
from jax.experimental.pallas import tpu as pltpu

D_MODEL = 1024
DEPTH = 4
N_HEADS = 8
N_KV_HEADS = 2
HEAD_DIM = 64
GQA_GROUP = N_HEADS // N_KV_HEADS
ATTN_W = N_HEADS * HEAD_DIM
KV_W = N_KV_HEADS * HEAD_DIM
POOL_WINDOWS = (2, 4, 8, 16)
N_POOL_GROUPS = 4
POOL_W = D_MODEL // 2
POOL_GROUP_W = POOL_W // N_POOL_GROUPS
MIX_W = ATTN_W + POOL_W
IN_W = ATTN_W + 2 * KV_W + POOL_W
D_FF = 2816
GRID_W = 64
WINDOW = 128
ROPE_BASE = 10000.0
LN_EPS = 1e-5
DEEPNORM_ALPHA = (2 * DEPTH) ** 0.25
ATTN_SCALE = HEAD_DIM ** -0.5
NEG_INF = -1e30
LOG2E = 1.4426950408889634

LANES = 128
SUBLANES = 8
TILE = 1024
FF_CHUNK = 256
N_FF_CHUNKS = D_FF // FF_CHUNK
EPI_ROWS = 512
OUT_ROWS = 256
ATTN_UNROLL = 2
MOD_COLS = 1536
MOD_ROWS = 16
VMEM_LIMIT = 60000 * 1024

F32 = jnp.float32
BF16 = jnp.bfloat16


def _silu(x):
    return x / (1.0 + jnp.exp(-x))


def _layer_norm(y, g, b):
    mu = jnp.mean(y, axis=-1, keepdims=True)
    yc = y - mu
    var = jnp.mean(yc * yc, axis=-1, keepdims=True)
    return yc * lax.rsqrt(var + LN_EPS) * g + b


def _mod_slice(mod_ref, j):
    return mod_ref[:, j * D_MODEL:(j + 1) * D_MODEL]


def _mm(a, w):
    return lax.dot_general(a, w, (((1,), (0,)), ((), ())), preferred_element_type=F32)


COL_BLOCKS = D_MODEL // LANES


def _cols_load(ref, r0, n):
    return jnp.concatenate([ref[j, r0:r0 + n, :] for j in range(COL_BLOCKS)], axis=1)


def _cols_store(ref, r0, y):
    for j in range(COL_BLOCKS):
        ref[j, r0:r0 + y.shape[0], :] = y[:, j * LANES:(j + 1) * LANES]


def _store_permuted(ref, y, row0, seq):
    per = seq // SUBLANES
    for q in range(y.shape[0] // per):
        t0 = row0 + q * per
        s, a = t0 // seq, (t0 % seq) // per
        for j in range(COL_BLOCKS):
            ref[j, pl.ds(s * seq + a, per, stride=SUBLANES), :] = (
                y[q * per:(q + 1) * per, j * LANES:(j + 1) * LANES])


def _store_natural(o_ref, stage_ref, y, row0, seq):
    n = y.shape[0]
    per = seq // SUBLANES
    _cols_store(stage_ref, 0, y)
    span = min(n, seq)
    for q0 in range(0, n, span):
        s, b0 = (row0 + q0) // seq, ((row0 + q0) % seq) // SUBLANES
        for a in range(SUBLANES):
            t0 = s * seq + a * per + b0
            for j in range(COL_BLOCKS):
                o_ref[t0:t0 + span // SUBLANES, j * LANES:(j + 1) * LANES] = (
                    stage_ref[j, pl.ds(q0 + a, span // SUBLANES, stride=SUBLANES), :])


def _cols_spec(tile_of=lambda i: i):
    return pl.BlockSpec((COL_BLOCKS, TILE, LANES), lambda i: (0, tile_of(i), 0))


def _whole_spec(shape):
    idx = (0,) * len(shape)
    return pl.BlockSpec(tuple(shape), lambda i: idx, pipeline_mode=pl.Buffered(1))


def _layer_spec(shape, layer):
    idx = (layer,) + (0,) * len(shape)
    return pl.BlockSpec((None,) + tuple(shape), lambda i: idx, pipeline_mode=pl.Buffered(1))


def _mod_kernel(cond_ref, w_ref, b_ref, o_ref):
    o_ref[...] = _mm(_silu(cond_ref[...]).astype(BF16), w_ref[...]) + b_ref[...]


def _mod_specs(layer, n_steps):
    cols = (6 * D_MODEL) // n_steps
    assert cols * n_steps == 6 * D_MODEL and cols % LANES == 0
    in_specs = [pl.BlockSpec((MOD_ROWS, D_MODEL), lambda j: (0, 0)),
                pl.BlockSpec((None, D_MODEL, cols), lambda j: (layer, 0, j)),
                pl.BlockSpec((None, 1, cols), lambda j: (layer, 0, j))]
    return in_specs, pl.BlockSpec((MOD_ROWS, cols), lambda j: (0, j))


def _modulation(cond, w_mod, b_mod, layer):
    n_steps = (6 * D_MODEL) // MOD_COLS
    in_specs, out_spec = _mod_specs(layer, n_steps)
    return pl.pallas_call(
        _mod_kernel,
        grid=(n_steps,),
        in_specs=in_specs,
        out_specs=out_spec,
        out_shape=jax.ShapeDtypeStruct((MOD_ROWS, 6 * D_MODEL), F32),
        compiler_params=pltpu.CompilerParams(
            dimension_semantics=("arbitrary",), vmem_limit_bytes=VMEM_LIMIT),
        name="modulation",
    )(cond, w_mod, b_mod)


def _dup_halves(x):
    lo = lax.broadcasted_iota(jnp.int32, x.shape, 1) < HEAD_DIM
    xr = pltpu.roll(x, HEAD_DIM, axis=1)
    return jnp.where(lo, x, xr), jnp.where(lo, xr, x)


def _softmax_attend(lhs, sink_col, key_sets):
    def lane_blocks(a):
        return [a[:, j * LANES:(j + 1) * LANES] for j in range(a.shape[1] // LANES)]

    scores = []
    for k2, _, bias in key_sets:
        s = lax.dot_general(lhs, k2, (((1,), (1,)), ((), ())), preferred_element_type=F32)
        if bias is not None:
            rows, cols = s.shape
            s = (s.reshape(GQA_GROUP, rows // GQA_GROUP, cols) + bias[None]).reshape(rows, cols)
        scores.append(s)
    m_lanes = functools.reduce(jnp.maximum, [b for s in scores for b in lane_blocks(s)])
    m = jnp.maximum(jnp.max(m_lanes, axis=-1, keepdims=True), sink_col)
    e_lanes = None
    out = None
    for s, (_, v2, _) in zip(scores, key_sets):
        e = jnp.exp2(s - m)
        e_sum = functools.reduce(jnp.add, lane_blocks(e))
        e_lanes = e_sum if e_lanes is None else e_lanes + e_sum
        o = jnp.dot(e.astype(BF16), v2, preferred_element_type=F32)
        out = o if out is None else out + o
    denom = jnp.sum(e_lanes, axis=-1, keepdims=True) + jnp.exp2(sink_col - m)
    return out / denom


def _mixer_kernel(*refs, latent, seq, layer, next_mods):
    if latent:
        (sink_ref, x_ref, mod_ref, w_in_ref, w_pool_ref, ps_ref, w_out_ref, g_ref, b_ref,
         kc_ref, vc_ref, rope_ref, wu_f32_ref, wd_f32_ref,
         x1_ref, wu_bf16_ref, wd_bf16_ref, q_s, k_s, v_s, mix_s, kc_s, vc_s) = refs
        wu_bf16_ref[...] = wu_f32_ref[...].astype(BF16)
        wd_bf16_ref[...] = wd_f32_ref[...].astype(BF16)
    elif next_mods:
        (sink_ref, x_ref, mod_ref, w_in_ref, w_pool_ref, ps_ref, w_out_ref, g_ref, b_ref,
         cond_ref, wm_ref, bm_ref,
         x1_ref, k_out_ref, v_out_ref, mods_next_ref, q_s, k_s, v_s, mix_s) = refs
        _mod_kernel(cond_ref, wm_ref, bm_ref, mods_next_ref)
    else:
        (sink_ref, x_ref, mod_ref, w_in_ref, w_pool_ref, ps_ref, w_out_ref, g_ref, b_ref,
         x1_ref, k_out_ref, v_out_ref, q_s, k_s, v_s, mix_s) = refs

    h = (x_ref[...] * (1.0 + _mod_slice(mod_ref, 1)) + _mod_slice(mod_ref, 0)).astype(BF16)
    qkv = _mm(h, w_in_ref[:, :ATTN_W + 2 * KV_W])
    p = _mm(h, w_in_ref[:, ATTN_W + 2 * KV_W:])
    k = qkv[:, ATTN_W:ATTN_W + KV_W]
    v = qkv[:, ATTN_W + KV_W:ATTN_W + 2 * KV_W]

    n_seq = TILE // seq
    sub = lax.broadcasted_iota(jnp.int32, (SUBLANES, POOL_GROUP_W), 0)

    def edge_rows(a, first_fn, last_fn):
        pieces = []
        for s in range(n_seq):
            blk = a[s * seq:(s + 1) * seq]
            head = blk[:SUBLANES] if first_fn is None else first_fn(blk[:SUBLANES])
            tail = blk[seq - SUBLANES:] if last_fn is None else last_fn(blk[seq - SUBLANES:])
            pieces += [head, blk[SUBLANES:seq - SUBLANES], tail]
        return jnp.concatenate(pieces, axis=0)

    def shift(a, j):
        rolled = pltpu.roll(a, (-j) % TILE, axis=0)
        if j < 0:
            return edge_rows(rolled, lambda r: jnp.where(sub >= -j, r, 0.0), None)
        return edge_rows(rolled, None, lambda r: jnp.where(sub < SUBLANES - j, r, 0.0))

    def pool_group(g):
        pg = p[:, g * POOL_GROUP_W:(g + 1) * POOL_GROUP_W]
        half = POOL_WINDOWS[g] // 2
        left, right, n = shift(pg, -1), pg, 1
        while n < half:
            left = left + shift(left, -n)
            right = right + shift(right, n)
            n *= 2
        total = left + right
        first_cnt = jnp.minimum(sub + half, 2 * half).astype(F32)
        last_cnt = jnp.minimum(SUBLANES - sub + half, 2 * half).astype(F32)
        mean = edge_rows(total * (1.0 / (2 * half)),
                         lambda r: r * (2.0 * half) / first_cnt,
                         lambda r: r * (2.0 * half) / last_cnt)
        d = (mean - pg).astype(BF16)
        y = _mm(d, w_pool_ref[g])
        y = y * ps_ref[:, g * POOL_GROUP_W:(g + 1) * POOL_GROUP_W]
        mix_s[:, ATTN_W + g * POOL_GROUP_W:ATTN_W + (g + 1) * POOL_GROUP_W] = y.astype(BF16)

    if latent:
        cos = rope_ref[0]
        sin_up = rope_ref[1]
        sin_dn = rope_ref[2]

        def rope(t):
            return (t * cos + pltpu.roll(t, LANES - HEAD_DIM // 4, axis=1) * sin_up
                    + pltpu.roll(t, HEAD_DIM // 4, axis=1) * sin_dn)

        k = rope(k)
        pad = WINDOW
        zeros = jnp.zeros((pad, LANES), BF16)
        for hh in range(N_KV_HEADS):
            k_s[hh, 0:pad, :] = zeros
            k_s[hh, pad + TILE:2 * pad + TILE, :] = zeros
            v_s[hh, 0:pad, :] = zeros
            v_s[hh, pad + TILE:2 * pad + TILE, :] = zeros
        kc0, kc1 = _dup_halves(kc_ref[...])
        vc0, vc1 = _dup_halves(vc_ref[...])
        kc_s[0] = kc0.astype(BF16)
        kc_s[1] = kc1.astype(BF16)
        vc_s[0] = vc0.astype(BF16)
        vc_s[1] = vc1.astype(BF16)
    else:
        pad = 0
        k_out_ref[...] = k
        v_out_ref[...] = v

    for j in range(ATTN_W // LANES):
        qj = qkv[:, j * LANES:(j + 1) * LANES]
        if latent:
            qj = rope(qj)
        q_s[:, j * LANES:(j + 1) * LANES] = (qj * (ATTN_SCALE * LOG2E)).astype(BF16)
        pool_group(j)
    k0, k1 = _dup_halves(k)
    v0, v1 = _dup_halves(v)
    k_s[0, pad:pad + TILE, :] = k0.astype(BF16)
    k_s[1, pad:pad + TILE, :] = k1.astype(BF16)
    v_s[0, pad:pad + TILE, :] = v0.astype(BF16)
    v_s[1, pad:pad + TILE, :] = v1.astype(BF16)

    qb = WINDOW if latent else seq
    n_rows = GQA_GROUP * qb
    row = lax.broadcasted_iota(jnp.int32, (n_rows, 1), 0)
    lo_q = lax.broadcasted_iota(jnp.int32, (qb, LANES), 1) < HEAD_DIM
    if latent:
        r = lax.broadcasted_iota(jnp.int32, (qb, 3 * WINDOW), 0)
        c = lax.broadcasted_iota(jnp.int32, (qb, 3 * WINDOW), 1)
        in_band = jnp.abs(r - c + WINDOW) <= WINDOW

    def attend_block(i, carry):
        r0 = pl.multiple_of(i * qb, qb)
        if latent:
            kpos = (i - 1) * WINDOW + c
            valid = in_band & (kpos >= 0) & (kpos < TILE)
            bias = jnp.where(valid, 0.0, NEG_INF * LOG2E).astype(F32)
        for hh in range(N_KV_HEADS):
            parts = []
            for pair in range(GQA_GROUP // 2):
                c0 = hh * GQA_GROUP * HEAD_DIM + pair * LANES
                qp = q_s[pl.ds(r0, qb), c0:c0 + LANES]
                parts.append(jnp.where(lo_q, qp, jnp.zeros_like(qp)))
                parts.append(jnp.where(lo_q, jnp.zeros_like(qp), qp))
            lhs = jnp.concatenate(parts, axis=0)
            h0 = hh * GQA_GROUP
            sink_col = jnp.full((n_rows, 1), sink_ref[layer, h0 + GQA_GROUP - 1] * LOG2E, F32)
            for g in range(GQA_GROUP - 2, -1, -1):
                sink_col = jnp.where(row < (g + 1) * qb, sink_ref[layer, h0 + g] * LOG2E, sink_col)
            if latent:
                key_sets = [(kc_s[hh], vc_s[hh], None),
                            (k_s[hh, pl.ds(r0, 3 * WINDOW), :], v_s[hh, pl.ds(r0, 3 * WINDOW), :], bias)]
            else:
                key_sets = [(k_s[hh, pl.ds(r0, qb), :], v_s[hh, pl.ds(r0, qb), :], None)]
            o = _softmax_attend(lhs, sink_col, key_sets)
            for pair in range(GQA_GROUP // 2):
                c0 = hh * GQA_GROUP * HEAD_DIM + pair * LANES
                oa = o[(2 * pair) * qb:(2 * pair + 1) * qb]
                ob = o[(2 * pair + 1) * qb:(2 * pair + 2) * qb]
                mix_s[pl.ds(r0, qb), c0:c0 + LANES] = jnp.where(lo_q, oa, ob).astype(BF16)
        return carry

    lax.fori_loop(0, TILE // qb, attend_block, 0, unroll=ATTN_UNROLL)

    for r0 in range(0, TILE, OUT_ROWS):
        mix = _mm(mix_s[r0:r0 + OUT_ROWS, :], w_out_ref[...])
        y = DEEPNORM_ALPHA * x_ref[r0:r0 + OUT_ROWS, :] + _mod_slice(mod_ref, 2) * mix
        _store_permuted(x1_ref, _layer_norm(y, g_ref[...], b_ref[...]), r0, seq)


def _mixer_call(x, mods, mod_row, sink, w_in, w_pool, pool_scale, w_out, ln_g, ln_b, *,
                layer, latent, seq, n_tiles, in_tile0, cache_k=None, cache_v=None, rope=None,
                ffn_f32=(), next_mod_params=None):
    n_tok = n_tiles * TILE
    in_specs = [
        pl.BlockSpec(memory_space=pltpu.SMEM),
        pl.BlockSpec((TILE, D_MODEL), lambda i: (i + in_tile0, 0)),
        pl.BlockSpec((None, 1, 6 * D_MODEL), lambda i: (mod_row(i), 0, 0)),
        _layer_spec((D_MODEL, IN_W), layer),
        _layer_spec((N_POOL_GROUPS, POOL_GROUP_W, POOL_GROUP_W), layer),
        _layer_spec((1, POOL_W), layer),
        _layer_spec((MIX_W, D_MODEL), layer),
        _layer_spec((1, D_MODEL), layer),
        _layer_spec((1, D_MODEL), layer),
    ]
    args = [sink, x, mods, w_in, w_pool, pool_scale, w_out, ln_g, ln_b]
    pad = WINDOW if latent else 0
    scratch = [
        pltpu.VMEM((TILE, ATTN_W), BF16),
        pltpu.VMEM((N_KV_HEADS, TILE + 2 * pad, LANES), BF16),
        pltpu.VMEM((N_KV_HEADS, TILE + 2 * pad, LANES), BF16),
        pltpu.VMEM((TILE, MIX_W), BF16),
    ]
    x_out = jax.ShapeDtypeStruct((COL_BLOCKS, n_tok, LANES), F32)
    x_spec = _cols_spec()
    if latent:
        past = cache_k.shape[2]
        in_specs += [
            pl.BlockSpec((None, None, past, KV_W), lambda i: (i, layer, 0, 0)),
            pl.BlockSpec((None, None, past, KV_W), lambda i: (i, layer, 0, 0)),
            pl.BlockSpec((3, TILE, LANES), lambda i: (0, 0, 0), pipeline_mode=pl.Buffered(1)),
        ]
        args += [cache_k, cache_v, rope]
        scratch += [pltpu.VMEM((N_KV_HEADS, past, LANES), BF16),
                    pltpu.VMEM((N_KV_HEADS, past, LANES), BF16)]
        out_shape, out_specs = [x_out], [x_spec]
        for w in ffn_f32:
            rows = w.shape[1] // n_tiles
            assert rows * n_tiles == w.shape[1] and rows % 16 == 0
            in_specs.append(pl.BlockSpec((None, rows, w.shape[2]), lambda i: (layer, i, 0)))
            args.append(w)
            out_shape.append(jax.ShapeDtypeStruct(w.shape[1:], BF16))
            out_specs.append(pl.BlockSpec((rows, w.shape[2]), lambda i: (i, 0)))
    else:
        kv_out = jax.ShapeDtypeStruct((n_tok, KV_W), F32)
        kv_spec = pl.BlockSpec((TILE, KV_W), lambda i: (i, 0))
        out_shape = [x_out, kv_out, kv_out]
        out_specs = [x_spec, kv_spec, kv_spec]
        if next_mod_params is not None:
            mod_in_specs, mod_out_spec = _mod_specs(layer + 1, n_tiles)
            in_specs += mod_in_specs
            args += list(next_mod_params)
            out_shape.append(jax.ShapeDtypeStruct((MOD_ROWS, 6 * D_MODEL), F32))
            out_specs.append(mod_out_spec)
    return pl.pallas_call(
        functools.partial(_mixer_kernel, latent=latent, seq=seq, layer=layer,
                          next_mods=next_mod_params is not None),
        grid=(n_tiles,),
        in_specs=in_specs,
        out_specs=out_specs,
        out_shape=out_shape,
        scratch_shapes=scratch,
        compiler_params=pltpu.CompilerParams(
            dimension_semantics=("arbitrary",), vmem_limit_bytes=VMEM_LIMIT),
        name="mixer_latent" if latent else "mixer_context",
    )(*args)


def _ffn_kernel(*refs, groups):
    x_refs, rest = refs[:len(groups)], refs[len(groups):]
    step = pl.program_id(0)
    tile0 = 0
    for x_ref, (n_tiles, seq) in zip(x_refs, groups):
        tile_fn = functools.partial(_ffn_tile, x_ref, *rest, seq=seq)
        if len(groups) == 1:
            tile_fn()
        else:
            pl.when((step >= tile0) & (step < tile0 + n_tiles))(tile_fn)
        tile0 += n_tiles


def _ffn_tile(x_ref, mod_ref, w_up_ref, cw_ref, cb_ref, w_down_ref, g_ref, b_ref, o_ref,
              h_s, u_s, act_s, stage_s, *, seq):
    n_seq = TILE // seq
    h_s[...] = (_cols_load(x_ref, 0, TILE) * (1.0 + _mod_slice(mod_ref, 4))
                + _mod_slice(mod_ref, 3)).astype(BF16)
    sub = lax.broadcasted_iota(jnp.int32, (SUBLANES, FF_CHUNK), 0)

    def up_project(c):
        hb = h_s[...]
        for part in range(2):
            col = part * D_FF + c * FF_CHUNK
            u_s[c % 2, part] = jnp.dot(hb, w_up_ref[:, col:col + FF_CHUNK],
                                       preferred_element_type=F32)

    def conv_swiglu(c):
        for s in range(n_seq):
            halves = []
            for part in range(2):
                col = part * D_FF + c * FF_CHUNK
                cw = cw_ref[:, col:col + FF_CHUNK]
                cur = u_s[c % 2, part, s * seq:(s + 1) * seq, :]
                wrap_prev = jnp.where(sub == 0, 0.0, pltpu.roll(cur[seq - SUBLANES:], 1, axis=0))
                wrap_next = jnp.where(sub == SUBLANES - 1, 0.0,
                                      pltpu.roll(cur[:SUBLANES], SUBLANES - 1, axis=0))
                prev = jnp.concatenate([wrap_prev, cur[:seq - SUBLANES]], axis=0)
                nxt = jnp.concatenate([cur[SUBLANES:], wrap_next], axis=0)
                halves.append(prev * cw[0:1] + cur * cw[1:2] + nxt * cw[2:3]
                              + cb_ref[:, col:col + FF_CHUNK])
            a, g = halves
            act_s[s * seq:(s + 1) * seq, c * FF_CHUNK:(c + 1) * FF_CHUNK] = (_silu(g) * a).astype(BF16)

    up_project(0)
    for c in range(N_FF_CHUNKS):
        if c + 1 < N_FF_CHUNKS:
            up_project(c + 1)
        conv_swiglu(c)

    for r0 in range(0, TILE, EPI_ROWS):
        ff = _mm(act_s[r0:r0 + EPI_ROWS, :], w_down_ref[...])
        y = DEEPNORM_ALPHA * _cols_load(x_ref, r0, EPI_ROWS) + _mod_slice(mod_ref, 5) * ff
        _store_natural(o_ref, stage_s, _layer_norm(y, g_ref[...], b_ref[...]), r0, seq)


def _ffn_call(xs, groups, mods, mod_row, w_up, conv_w, conv_b, w_down, ln_g, ln_b, *, layer,
              name):
    n_steps = sum(n for n, _ in groups)
    scratch = [pltpu.VMEM((TILE, D_MODEL), BF16),
               pltpu.VMEM((2, 2, TILE, FF_CHUNK), F32),
               pltpu.VMEM((TILE, D_FF), BF16),
               pltpu.VMEM((COL_BLOCKS, EPI_ROWS, LANES), F32)]
    out_specs = pl.BlockSpec((TILE, D_MODEL), lambda i: (i, 0))
    out_shape = jax.ShapeDtypeStruct((n_steps * TILE, D_MODEL), F32)
    x_specs, tile0 = [], 0
    for n_tiles, _ in groups:
        x_specs.append(_cols_spec(
            lambda i, tile0=tile0, n_tiles=n_tiles: jnp.clip(i - tile0, 0, n_tiles - 1)))
        tile0 += n_tiles
    return pl.pallas_call(
        functools.partial(_ffn_kernel, groups=tuple(groups)),
        grid=(n_steps,),
        in_specs=x_specs + [
            pl.BlockSpec((None, 1, 6 * D_MODEL), lambda i: (mod_row(i), 0, 0)),
            _whole_spec((D_MODEL, 2 * D_FF)),
            _layer_spec((3, 2 * D_FF), layer),
            _layer_spec((1, 2 * D_FF), layer),
            _whole_spec((D_FF, D_MODEL)),
            _layer_spec((1, D_MODEL), layer),
            _layer_spec((1, D_MODEL), layer),
        ],
        out_specs=out_specs,
        out_shape=out_shape,
        scratch_shapes=scratch,
        compiler_params=pltpu.CompilerParams(
            dimension_semantics=("arbitrary",), vmem_limit_bytes=VMEM_LIMIT),
        name=name,
    )(*xs, mods, w_up, conv_w, conv_b, w_down, ln_g, ln_b)


def _rope_tables(n_pos):
    half = HEAD_DIM // 2
    t = jnp.arange(n_pos)
    inv_freq = ROPE_BASE ** (-jnp.arange(0, half, 2, dtype=F32) / half)

    def ang(p):
        a = p.astype(F32)[:, None] * inv_freq[None, :]
        return jnp.concatenate([a, a], axis=-1)

    a = jnp.concatenate([ang(t // GRID_W), ang(t % GRID_W)], axis=-1)
    cos = jnp.cos(a)
    sin = jnp.sin(a)
    first = (jnp.arange(HEAD_DIM) % half) < half // 2
    sin_up = jnp.where(first[None, :], -sin, 0.0)
    sin_dn = jnp.where(first[None, :], 0.0, sin)
    tabs = jnp.stack([cos, sin_up, sin_dn])
    return jnp.concatenate([tabs, tabs], axis=-1).astype(F32)


def kernel(x_prompt, x_sample, cache_k, cache_v, c, c_ctx, w_mod, b_mod, w_in, attn_sink,
           w_pool, pool_scale, w_out, ln1_g, ln1_b, w_up, conv_w, conv_b, w_down, ln2_g, ln2_b):
    batch, seq, _ = x_prompt.shape
    dec_batch, dec_seq, _ = x_sample.shape
    past = cache_k.shape[2]
    assert TILE % seq == 0 and dec_seq == TILE and seq & (seq - 1) == 0
    assert (batch * seq) % TILE == 0 and past % 16 == 0

    cond = jnp.zeros((MOD_ROWS, D_MODEL), F32).at[0].set(c_ctx).at[1:1 + dec_batch].set(c)
    mod_params = (cond, w_mod, b_mod.reshape(DEPTH, 1, 6 * D_MODEL))
    mods = _modulation(*mod_params, layer=0)

    mixer_w = (attn_sink, w_in, w_pool,
               pool_scale.reshape(DEPTH, 1, POOL_W), w_out,
               ln1_g.reshape(DEPTH, 1, D_MODEL), ln1_b.reshape(DEPTH, 1, D_MODEL))
    conv_b_r = conv_b.reshape(DEPTH, 1, 2 * D_FF)
    ln2_g_r, ln2_b_r = ln2_g.reshape(DEPTH, 1, D_MODEL), ln2_b.reshape(DEPTH, 1, D_MODEL)
    rope = _rope_tables(dec_seq)
    cache_k_r = cache_k.reshape(dec_batch, DEPTH, past, KV_W)
    cache_v_r = cache_v.reshape(dec_batch, DEPTH, past, KV_W)

    xc = x_prompt.reshape(batch * seq, D_MODEL)
    xl = x_sample.reshape(dec_batch * dec_seq, D_MODEL)
    ctx_tiles = batch * seq // TILE
    lat_tiles = dec_batch * dec_seq // TILE
    ctx_row = lambda i: 0
    lat_row = lambda i: i + 1
    ctx_group, lat_group = (ctx_tiles, seq), (lat_tiles, dec_seq)
    ks, vs = [], []
    for l in range(DEPTH):
        mods_l = mods.reshape(MOD_ROWS, 1, 6 * D_MODEL)
        ctx_out = _mixer_call(xc, mods_l, ctx_row, *mixer_w, layer=l, latent=False, seq=seq,
                              n_tiles=ctx_tiles, in_tile0=0,
                              next_mod_params=mod_params if l + 1 < DEPTH else None)
        xc, k_l, v_l = ctx_out[:3]
        mods = ctx_out[3] if l + 1 < DEPTH else None
        xl, w_up_b, w_down_b = _mixer_call(
            xl, mods_l, lat_row, *mixer_w, layer=l, latent=True, seq=dec_seq, n_tiles=lat_tiles,
            in_tile0=0, cache_k=cache_k_r, cache_v=cache_v_r, rope=rope, ffn_f32=(w_up, w_down))
        ffn_w = (w_up_b, conv_w, conv_b_r, w_down_b, ln2_g_r, ln2_b_r)
        xc = _ffn_call([xc], [ctx_group], mods_l, ctx_row, *ffn_w, layer=l, name="ffn_context")
        xl = _ffn_call([xl], [lat_group], mods_l, lat_row, *ffn_w, layer=l, name="ffn_latent")
        ks.append(k_l.reshape(batch, seq, KV_W))
        vs.append(v_l.reshape(batch, seq, KV_W))
    y_prompt = xc.reshape(batch, seq, D_MODEL)
    y_sample = xl.reshape(dec_batch, dec_seq, D_MODEL)
    cache_shape = (batch, DEPTH, seq, N_KV_HEADS, HEAD_DIM)
    return (y_prompt, y_sample, jnp.stack(ks, axis=1).reshape(cache_shape),
            jnp.stack(vs, axis=1).reshape(cache_shape))
```

```python
import functools

import jax
import jax.numpy as jnp
from jax import lax
from jax.experimental import pallas as pl
from jax.experimental.pallas import tpu as pltpu

D_MODEL = 1024
DEPTH = 4
N_HEADS = 8
N_KV_HEADS = 2
HEAD_DIM = 64
GQA_GROUP = N_HEADS // N_KV_HEADS
ATTN_W = N_HEADS * HEAD_DIM
KV_W = N_KV_HEADS * HEAD_DIM
POOL_WINDOWS = (2, 4, 8, 16)
N_POOL_GROUPS = 4
POOL_W = D_MODEL // 2
POOL_GROUP_W = POOL_W // N_POOL_GROUPS
MIX_W = ATTN_W + POOL_W
IN_W = ATTN_W + 2 * KV_W + POOL_W
D_FF = 2816
GRID_W = 64
WINDOW = 128
ROPE_BASE = 10000.0
LN_EPS = 1e-5
DEEPNORM_ALPHA = (2 * DEPTH) ** 0.25
ATTN_SCALE = HEAD_DIM ** -0.5
NEG_INF = -1e30
LOG2E = 1.4426950408889634

LANES = 128
SUBLANES = 8
TILE = 1024
FF_CHUNK = 256
N_FF_CHUNKS = D_FF // FF_CHUNK
EPI_ROWS = 256
OUT_ROWS = 256
LATENT_ATTN_UNROLL = 4
CONTEXT_ATTN_UNROLL = 2
MOD_COLS = 1536
MOD_ROWS = 16
VMEM_LIMIT = 60000 * 1024

F32 = jnp.float32
BF16 = jnp.bfloat16


def _silu(x):
    return x / (1.0 + jnp.exp(-x))


def _layer_norm(y, g, b):
    mu = jnp.mean(y, axis=-1, keepdims=True)
    yc = y - mu
    var = jnp.mean(yc * yc, axis=-1, keepdims=True)
    return yc * lax.rsqrt(var + LN_EPS) * g + b


def _mod_slice(mod_ref, j):
    return mod_ref[:, j * D_MODEL:(j + 1) * D_MODEL]


def _mm(a, w):
    return lax.dot_general(a, w, (((1,), (0,)), ((), ())), preferred_element_type=F32)


COL_BLOCKS = D_MODEL // LANES


def _cols_load(ref, r0, n):
    return jnp.concatenate([ref[j, r0:r0 + n, :] for j in range(COL_BLOCKS)], axis=1)


def _cols_store(ref, r0, y):
    for j in range(COL_BLOCKS):
        ref[j, r0:r0 + y.shape[0], :] = y[:, j * LANES:(j + 1) * LANES]


def _store_permuted(ref, y, row0, seq):
    per = seq // SUBLANES
    for q in range(y.shape[0] // per):
        t0 = row0 + q * per
        s, a = t0 // seq, (t0 % seq) // per
        for j in range(COL_BLOCKS):
            ref[j, pl.ds(s * seq + a, per, stride=SUBLANES), :] = (
                y[q * per:(q + 1) * per, j * LANES:(j + 1) * LANES])


def _store_natural(o_ref, stage_ref, y, row0, seq):
    n = y.shape[0]
    per = seq // SUBLANES
    _cols_store(stage_ref, 0, y)
    span = min(n, seq)
    for q0 in range(0, n, span):
        s, b0 = (row0 + q0) // seq, ((row0 + q0) % seq) // SUBLANES
        for a in range(SUBLANES):
            t0 = s * seq + a * per + b0
            for j in range(COL_BLOCKS):
                o_ref[t0:t0 + span // SUBLANES, j * LANES:(j + 1) * LANES] = (
                    stage_ref[j, pl.ds(q0 + a, span // SUBLANES, stride=SUBLANES), :])


def _cols_spec(tile_of=lambda i: i):
    return pl.BlockSpec((COL_BLOCKS, TILE, LANES), lambda i: (0, tile_of(i), 0))


def _whole_spec(shape):
    idx = (0,) * len(shape)
    return pl.BlockSpec(tuple(shape), lambda i: idx, pipeline_mode=pl.Buffered(1))


def _layer_spec(shape, layer):
    idx = (layer,) + (0,) * len(shape)
    return pl.BlockSpec((None,) + tuple(shape), lambda i: idx, pipeline_mode=pl.Buffered(1))


def _mod_kernel(cond_ref, w_ref, b_ref, o_ref):
    o_ref[...] = _mm(_silu(cond_ref[...]).astype(BF16), w_ref[...]) + b_ref[...]


def _mod_specs(layer, n_steps):
    cols = (6 * D_MODEL) // n_steps
    assert cols * n_steps == 6 * D_MODEL and cols % LANES == 0
    in_specs = [pl.BlockSpec((MOD_ROWS, D_MODEL), lambda j: (0, 0)),
                pl.BlockSpec((None, D_MODEL, cols), lambda j: (layer, 0, j)),
                pl.BlockSpec((None, 1, cols), lambda j: (layer, 0, j))]
    return in_specs, pl.BlockSpec((MOD_ROWS, cols), lambda j: (0, j))


def _modulation(cond, w_mod, b_mod, layer):
    n_steps = (6 * D_MODEL) // MOD_COLS
    in_specs, out_spec = _mod_specs(layer, n_steps)
    return pl.pallas_call(
        _mod_kernel,
        grid=(n_steps,),
        in_specs=in_specs,
        out_specs=out_spec,
        out_shape=jax.ShapeDtypeStruct((MOD_ROWS, 6 * D_MODEL), F32),
        compiler_params=pltpu.CompilerParams(
            dimension_semantics=("arbitrary",), vmem_limit_bytes=VMEM_LIMIT),
        name="modulation",
    )(cond, w_mod, b_mod)


def _dup_halves(x):
    lo = lax.broadcasted_iota(jnp.int32, x.shape, 1) < HEAD_DIM
    xr = pltpu.roll(x, HEAD_DIM, axis=1)
    return jnp.where(lo, x, xr), jnp.where(lo, xr, x)


def _attn_scores(lhs, key_sets):
    scores = []
    for k2, _, bias in key_sets:
        s = lax.dot_general(lhs, k2, (((1,), (1,)), ((), ())), preferred_element_type=F32)
        if bias is not None:
            rows, cols = s.shape
            s = (s.reshape(GQA_GROUP, rows // GQA_GROUP, cols) + bias[None]).reshape(rows, cols)
        scores.append(s)
    return scores


def _softmax_pv(scores, sink_col, key_sets):
    def lane_blocks(a):
        return [a[:, j * LANES:(j + 1) * LANES] for j in range(a.shape[1] // LANES)]

    m_lanes = functools.reduce(jnp.maximum, [b for s in scores for b in lane_blocks(s)])
    m = jnp.maximum(jnp.max(m_lanes, axis=-1, keepdims=True), sink_col)
    e_lanes = None
    out = None
    for s, (_, v2, _) in zip(scores, key_sets):
        e = jnp.exp2(s - m)
        e_sum = functools.reduce(jnp.add, lane_blocks(e))
        e_lanes = e_sum if e_lanes is None else e_lanes + e_sum
        o = jnp.dot(e.astype(BF16), v2, preferred_element_type=F32)
        out = o if out is None else out + o
    denom = jnp.sum(e_lanes, axis=-1, keepdims=True) + jnp.exp2(sink_col - m)
    return out / denom


def _mixer_kernel(*refs, latent, seq, layer, next_mods):
    if latent:
        (sink_ref, x_ref, mod_ref, w_in_ref, w_pool_ref, ps_ref, w_out_ref, g_ref, b_ref,
         kc_ref, vc_ref, rope_ref, wu_f32_ref, wd_f32_ref,
         x1_ref, wu_bf16_ref, wd_bf16_ref, q_s, k_s, v_s, mix_s, kc_s, vc_s) = refs
        wu_bf16_ref[...] = wu_f32_ref[...].astype(BF16)
        wd_bf16_ref[...] = wd_f32_ref[...].astype(BF16)
    elif next_mods:
        (sink_ref, x_ref, mod_ref, w_in_ref, w_pool_ref, ps_ref, w_out_ref, g_ref, b_ref,
         cond_ref, wm_ref, bm_ref,
         x1_ref, k_out_ref, v_out_ref, mods_next_ref, q_s, k_s, v_s, mix_s) = refs
        _mod_kernel(cond_ref, wm_ref, bm_ref, mods_next_ref)
    else:
        (sink_ref, x_ref, mod_ref, w_in_ref, w_pool_ref, ps_ref, w_out_ref, g_ref, b_ref,
         x1_ref, k_out_ref, v_out_ref, q_s, k_s, v_s, mix_s) = refs

    h = (x_ref[...] * (1.0 + _mod_slice(mod_ref, 1)) + _mod_slice(mod_ref, 0)).astype(BF16)
    qkv = _mm(h, w_in_ref[:, :ATTN_W + 2 * KV_W])
    p = _mm(h, w_in_ref[:, ATTN_W + 2 * KV_W:])
    k = qkv[:, ATTN_W:ATTN_W + KV_W]
    v = qkv[:, ATTN_W + KV_W:ATTN_W + 2 * KV_W]

    n_seq = TILE // seq
    sub = lax.broadcasted_iota(jnp.int32, (SUBLANES, POOL_GROUP_W), 0)

    def edge_rows(a, first_fn, last_fn):
        pieces = []
        for s in range(n_seq):
            blk = a[s * seq:(s + 1) * seq]
            head = blk[:SUBLANES] if first_fn is None else first_fn(blk[:SUBLANES])
            tail = blk[seq - SUBLANES:] if last_fn is None else last_fn(blk[seq - SUBLANES:])
            pieces += [head, blk[SUBLANES:seq - SUBLANES], tail]
        return jnp.concatenate(pieces, axis=0)

    def shift(a, j):
        rolled = pltpu.roll(a, (-j) % TILE, axis=0)
        if j < 0:
            return edge_rows(rolled, lambda r: jnp.where(sub >= -j, r, 0.0), None)
        return edge_rows(rolled, None, lambda r: jnp.where(sub < SUBLANES - j, r, 0.0))

    def pool_group(g):
        pg = p[:, g * POOL_GROUP_W:(g + 1) * POOL_GROUP_W]
        half = POOL_WINDOWS[g] // 2
        left, right, n = shift(pg, -1), pg, 1
        while n < half:
            left = left + shift(left, -n)
            right = right + shift(right, n)
            n *= 2
        total = left + right
        first_cnt = jnp.minimum(sub + half, 2 * half).astype(F32)
        last_cnt = jnp.minimum(SUBLANES - sub + half, 2 * half).astype(F32)
        mean = edge_rows(total * (1.0 / (2 * half)),
                         lambda r: r * (2.0 * half) / first_cnt,
                         lambda r: r * (2.0 * half) / last_cnt)
        d = (mean - pg).astype(BF16)
        y = _mm(d, w_pool_ref[g])
        y = y * ps_ref[:, g * POOL_GROUP_W:(g + 1) * POOL_GROUP_W]
        mix_s[:, ATTN_W + g * POOL_GROUP_W:ATTN_W + (g + 1) * POOL_GROUP_W] = y.astype(BF16)

    if latent:
        cos = rope_ref[0]
        sin_up = rope_ref[1]
        sin_dn = rope_ref[2]

        def rope(t):
            return (t * cos + pltpu.roll(t, LANES - HEAD_DIM // 4, axis=1) * sin_up
                    + pltpu.roll(t, HEAD_DIM // 4, axis=1) * sin_dn)

        k = rope(k)
        pad = WINDOW
        zeros = jnp.zeros((pad, LANES), BF16)
        for hh in range(N_KV_HEADS):
            k_s[hh, 0:pad, :] = zeros
            k_s[hh, pad + TILE:2 * pad + TILE, :] = zeros
            v_s[hh, 0:pad, :] = zeros
            v_s[hh, pad + TILE:2 * pad + TILE, :] = zeros
        kc0, kc1 = _dup_halves(kc_ref[...])
        vc0, vc1 = _dup_halves(vc_ref[...])
        kc_s[0] = kc0.astype(BF16)
        kc_s[1] = kc1.astype(BF16)
        vc_s[0] = vc0.astype(BF16)
        vc_s[1] = vc1.astype(BF16)
    else:
        pad = 0
        k_out_ref[...] = k
        v_out_ref[...] = v

    for j in range(ATTN_W // LANES):
        qj = qkv[:, j * LANES:(j + 1) * LANES]
        if latent:
            qj = rope(qj)
        q_s[:, j * LANES:(j + 1) * LANES] = (qj * (ATTN_SCALE * LOG2E)).astype(BF16)
        pool_group(j)
    k0, k1 = _dup_halves(k)
    v0, v1 = _dup_halves(v)
    k_s[0, pad:pad + TILE, :] = k0.astype(BF16)
    k_s[1, pad:pad + TILE, :] = k1.astype(BF16)
    v_s[0, pad:pad + TILE, :] = v0.astype(BF16)
    v_s[1, pad:pad + TILE, :] = v1.astype(BF16)

    qb = WINDOW if latent else seq
    n_rows = GQA_GROUP * qb
    row = lax.broadcasted_iota(jnp.int32, (n_rows, 1), 0)
    lo_q = lax.broadcasted_iota(jnp.int32, (qb, LANES), 1) < HEAD_DIM
    if latent:
        r = lax.broadcasted_iota(jnp.int32, (qb, 3 * WINDOW), 0)
        c = lax.broadcasted_iota(jnp.int32, (qb, 3 * WINDOW), 1)
        in_band = jnp.abs(r - c + WINDOW) <= WINDOW

    def key_sets_of(i, hh):
        r0 = pl.multiple_of(i * qb, qb)
        if not latent:
            return [(k_s[hh, pl.ds(r0, qb), :], v_s[hh, pl.ds(r0, qb), :], None)]
        kpos = (i - 1) * WINDOW + c
        valid = in_band & (kpos >= 0) & (kpos < TILE)
        bias = jnp.where(valid, 0.0, NEG_INF * LOG2E).astype(F32)
        return [(kc_s[hh], vc_s[hh], None),
                (k_s[hh, pl.ds(r0, 3 * WINDOW), :], v_s[hh, pl.ds(r0, 3 * WINDOW), :], bias)]

    def scores_of(i, hh):
        r0 = pl.multiple_of(i * qb, qb)
        parts = []
        for pair in range(GQA_GROUP // 2):
            c0 = hh * GQA_GROUP * HEAD_DIM + pair * LANES
            qp = q_s[pl.ds(r0, qb), c0:c0 + LANES]
            parts.append(jnp.where(lo_q, qp, jnp.zeros_like(qp)))
            parts.append(jnp.where(lo_q, jnp.zeros_like(qp), qp))
        lhs = jnp.concatenate(parts, axis=0)
        return _attn_scores(lhs, key_sets_of(i, hh))

    def finish(i, hh, scores):
        r0 = pl.multiple_of(i * qb, qb)
        h0 = hh * GQA_GROUP
        sink_col = jnp.full((n_rows, 1), sink_ref[layer, h0 + GQA_GROUP - 1] * LOG2E, F32)
        for g in range(GQA_GROUP - 2, -1, -1):
            sink_col = jnp.where(row < (g + 1) * qb, sink_ref[layer, h0 + g] * LOG2E, sink_col)
        o = _softmax_pv(scores, sink_col, key_sets_of(i, hh))
        for pair in range(GQA_GROUP // 2):
            c0 = hh * GQA_GROUP * HEAD_DIM + pair * LANES
            oa = o[(2 * pair) * qb:(2 * pair + 1) * qb]
            ob = o[(2 * pair + 1) * qb:(2 * pair + 2) * qb]
            mix_s[pl.ds(r0, qb), c0:c0 + LANES] = jnp.where(lo_q, oa, ob).astype(BF16)

    unroll = LATENT_ATTN_UNROLL if latent else CONTEXT_ATTN_UNROLL

    def attend_blocks(it, carry):
        groups = [(it * unroll + u, hh) for u in range(unroll) for hh in range(N_KV_HEADS)]
        if latent:
            scores = scores_of(*groups[0])
            for g, group in enumerate(groups):
                cur = scores
                if g + 1 < len(groups):
                    scores = scores_of(*groups[g + 1])
                finish(*group, cur)
        else:
            for group in groups:
                finish(*group, scores_of(*group))
        return carry

    lax.fori_loop(0, TILE // (qb * unroll), attend_blocks, 0)

    def out_project(r0):
        return _mm(mix_s[r0:r0 + OUT_ROWS, :], w_out_ref[...])

    mix_next = out_project(0)
    for r0 in range(0, TILE, OUT_ROWS):
        mix = mix_next
        if r0 + OUT_ROWS < TILE:
            mix_next = out_project(r0 + OUT_ROWS)
        y = DEEPNORM_ALPHA * x_ref[r0:r0 + OUT_ROWS, :] + _mod_slice(mod_ref, 2) * mix
        _store_permuted(x1_ref, _layer_norm(y, g_ref[...], b_ref[...]), r0, seq)


def _mixer_call(x, mods, mod_row, sink, w_in, w_pool, pool_scale, w_out, ln_g, ln_b, *,
                layer, latent, seq, n_tiles, in_tile0, cache_k=None, cache_v=None, rope=None,
                ffn_f32=(), next_mod_params=None):
    n_tok = n_tiles * TILE
    in_specs = [
        pl.BlockSpec(memory_space=pltpu.SMEM),
        pl.BlockSpec((TILE, D_MODEL), lambda i: (i + in_tile0, 0)),
        pl.BlockSpec((None, 1, 6 * D_MODEL), lambda i: (mod_row(i), 0, 0)),
        _layer_spec((D_MODEL, IN_W), layer),
        _layer_spec((N_POOL_GROUPS, POOL_GROUP_W, POOL_GROUP_W), layer),
        _layer_spec((1, POOL_W), layer),
        _layer_spec((MIX_W, D_MODEL), layer),
        _layer_spec((1, D_MODEL), layer),
        _layer_spec((1, D_MODEL), layer),
    ]
    args = [sink, x, mods, w_in, w_pool, pool_scale, w_out, ln_g, ln_b]
    pad = WINDOW if latent else 0
    scratch = [
        pltpu.VMEM((TILE, ATTN_W), BF16),
        pltpu.VMEM((N_KV_HEADS, TILE + 2 * pad, LANES), BF16),
        pltpu.VMEM((N_KV_HEADS, TILE + 2 * pad, LANES), BF16),
        pltpu.VMEM((TILE, MIX_W), BF16),
    ]
    x_out = jax.ShapeDtypeStruct((COL_BLOCKS, n_tok, LANES), F32)
    x_spec = _cols_spec()
    if latent:
        past = cache_k.shape[2]
        in_specs += [
            pl.BlockSpec((None, None, past, KV_W), lambda i: (i, layer, 0, 0)),
            pl.BlockSpec((None, None, past, KV_W), lambda i: (i, layer, 0, 0)),
            pl.BlockSpec((3, TILE, LANES), lambda i: (0, 0, 0), pipeline_mode=pl.Buffered(1)),
        ]
        args += [cache_k, cache_v, rope]
        scratch += [pltpu.VMEM((N_KV_HEADS, past, LANES), BF16),
                    pltpu.VMEM((N_KV_HEADS, past, LANES), BF16)]
        out_shape, out_specs = [x_out], [x_spec]
        for w in ffn_f32:
            rows = w.shape[1] // n_tiles
            assert rows * n_tiles == w.shape[1] and rows % 16 == 0
            in_specs.append(pl.BlockSpec((None, rows, w.shape[2]), lambda i: (layer, i, 0)))
            args.append(w)
            out_shape.append(jax.ShapeDtypeStruct(w.shape[1:], BF16))
            out_specs.append(pl.BlockSpec((rows, w.shape[2]), lambda i: (i, 0)))
    else:
        kv_out = jax.ShapeDtypeStruct((n_tok, KV_W), F32)
        kv_spec = pl.BlockSpec((TILE, KV_W), lambda i: (i, 0))
        out_shape = [x_out, kv_out, kv_out]
        out_specs = [x_spec, kv_spec, kv_spec]
        if next_mod_params is not None:
            mod_in_specs, mod_out_spec = _mod_specs(layer + 1, n_tiles)
            in_specs += mod_in_specs
            args += list(next_mod_params)
            out_shape.append(jax.ShapeDtypeStruct((MOD_ROWS, 6 * D_MODEL), F32))
            out_specs.append(mod_out_spec)
    return pl.pallas_call(
        functools.partial(_mixer_kernel, latent=latent, seq=seq, layer=layer,
                          next_mods=next_mod_params is not None),
        grid=(n_tiles,),
        in_specs=in_specs,
        out_specs=out_specs,
        out_shape=out_shape,
        scratch_shapes=scratch,
        compiler_params=pltpu.CompilerParams(
            dimension_semantics=("arbitrary",), vmem_limit_bytes=VMEM_LIMIT),
        name="mixer_latent" if latent else "mixer_context",
    )(*args)


def _ffn_kernel(*refs, groups):
    x_refs, rest = refs[:len(groups)], refs[len(groups):]
    step = pl.program_id(0)
    tile0 = 0
    for x_ref, (n_tiles, seq) in zip(x_refs, groups):
        tile_fn = functools.partial(_ffn_tile, x_ref, *rest, seq=seq)
        if len(groups) == 1:
            tile_fn()
        else:
            pl.when((step >= tile0) & (step < tile0 + n_tiles))(tile_fn)
        tile0 += n_tiles


def _ffn_tile(x_ref, mod_ref, w_up_ref, cw_ref, cb_ref, w_down_ref, g_ref, b_ref, o_ref,
              h_s, u_s, act_s, stage_s, *, seq):
    n_seq = TILE // seq
    h_s[...] = (_cols_load(x_ref, 0, TILE) * (1.0 + _mod_slice(mod_ref, 4))
                + _mod_slice(mod_ref, 3)).astype(BF16)
    sub = lax.broadcasted_iota(jnp.int32, (SUBLANES, FF_CHUNK), 0)

    def up_project(c):
        hb = h_s[...]
        for part in range(2):
            col = part * D_FF + c * FF_CHUNK
            u_s[c % 2, part] = jnp.dot(hb, w_up_ref[:, col:col + FF_CHUNK],
                                       preferred_element_type=F32)

    def conv_swiglu(c):
        for s in range(n_seq):
            halves = []
            for part in range(2):
                col = part * D_FF + c * FF_CHUNK
                cw = cw_ref[:, col:col + FF_CHUNK]
                cur = u_s[c % 2, part, s * seq:(s + 1) * seq, :]
                wrap_prev = jnp.where(sub == 0, 0.0, pltpu.roll(cur[seq - SUBLANES:], 1, axis=0))
                wrap_next = jnp.where(sub == SUBLANES - 1, 0.0,
                                      pltpu.roll(cur[:SUBLANES], SUBLANES - 1, axis=0))
                prev = jnp.concatenate([wrap_prev, cur[:seq - SUBLANES]], axis=0)
                nxt = jnp.concatenate([cur[SUBLANES:], wrap_next], axis=0)
                halves.append(prev * cw[0:1] + cur * cw[1:2] + nxt * cw[2:3]
                              + cb_ref[:, col:col + FF_CHUNK])
            a, g = halves
            act_s[s * seq:(s + 1) * seq, c * FF_CHUNK:(c + 1) * FF_CHUNK] = (_silu(g) * a).astype(BF16)

    up_project(0)
    for c in range(N_FF_CHUNKS):
        if c + 1 < N_FF_CHUNKS:
            up_project(c + 1)
        conv_swiglu(c)

    def down_project(r0):
        return _mm(act_s[r0:r0 + EPI_ROWS, :], w_down_ref[...])

    ff_next = down_project(0)
    for r0 in range(0, TILE, EPI_ROWS):
        ff = ff_next
        if r0 + EPI_ROWS < TILE:
            ff_next = down_project(r0 + EPI_ROWS)
        y = DEEPNORM_ALPHA * _cols_load(x_ref, r0, EPI_ROWS) + _mod_slice(mod_ref, 5) * ff
        _store_natural(o_ref, stage_s, _layer_norm(y, g_ref[...], b_ref[...]), r0, seq)


def _ffn_call(xs, groups, mods, mod_row, w_up, conv_w, conv_b, w_down, ln_g, ln_b, *, layer,
              name):
    n_steps = sum(n for n, _ in groups)
    scratch = [pltpu.VMEM((TILE, D_MODEL), BF16),
               pltpu.VMEM((2, 2, TILE, FF_CHUNK), F32),
               pltpu.VMEM((TILE, D_FF), BF16),
               pltpu.VMEM((COL_BLOCKS, EPI_ROWS, LANES), F32)]
    out_specs = pl.BlockSpec((TILE, D_MODEL), lambda i: (i, 0))
    out_shape = jax.ShapeDtypeStruct((n_steps * TILE, D_MODEL), F32)
    x_specs, tile0 = [], 0
    for n_tiles, _ in groups:
        x_specs.append(_cols_spec(
            lambda i, tile0=tile0, n_tiles=n_tiles: jnp.clip(i - tile0, 0, n_tiles - 1)))
        tile0 += n_tiles
    return pl.pallas_call(
        functools.partial(_ffn_kernel, groups=tuple(groups)),
        grid=(n_steps,),
        in_specs=x_specs + [
            pl.BlockSpec((None, 1, 6 * D_MODEL), lambda i: (mod_row(i), 0, 0)),
            _whole_spec((D_MODEL, 2 * D_FF)),
            _layer_spec((3, 2 * D_FF), layer),
            _layer_spec((1, 2 * D_FF), layer),
            _whole_spec((D_FF, D_MODEL)),
            _layer_spec((1, D_MODEL), layer),
            _layer_spec((1, D_MODEL), layer),
        ],
        out_specs=out_specs,
        out_shape=out_shape,
        scratch_shapes=scratch,
        compiler_params=pltpu.CompilerParams(
            dimension_semantics=("arbitrary",), vmem_limit_bytes=VMEM_LIMIT),
        name=name,
    )(*xs, mods, w_up, conv_w, conv_b, w_down, ln_g, ln_b)


def _rope_tables(n_pos):
    half = HEAD_DIM // 2
    t = jnp.arange(n_pos)
    inv_freq = ROPE_BASE ** (-jnp.arange(0, half, 2, dtype=F32) / half)

    def ang(p):
        a = p.astype(F32)[:, None] * inv_freq[None, :]
        return jnp.concatenate([a, a], axis=-1)

    a = jnp.concatenate([ang(t // GRID_W), ang(t % GRID_W)], axis=-1)
    cos = jnp.cos(a)
    sin = jnp.sin(a)
    first = (jnp.arange(HEAD_DIM) % half) < half // 2
    sin_up = jnp.where(first[None, :], -sin, 0.0)
    sin_dn = jnp.where(first[None, :], 0.0, sin)
    tabs = jnp.stack([cos, sin_up, sin_dn])
    return jnp.concatenate([tabs, tabs], axis=-1).astype(F32)


def kernel(x_prompt, x_sample, cache_k, cache_v, c, c_ctx, w_mod, b_mod, w_in, attn_sink,
           w_pool, pool_scale, w_out, ln1_g, ln1_b, w_up, conv_w, conv_b, w_down, ln2_g, ln2_b):
    batch, seq, _ = x_prompt.shape
    dec_batch, dec_seq, _ = x_sample.shape
    past = cache_k.shape[2]
    assert TILE % seq == 0 and dec_seq == TILE and seq & (seq - 1) == 0
    assert (batch * seq) % TILE == 0 and past % 16 == 0

    cond = jnp.zeros((MOD_ROWS, D_MODEL), F32).at[0].set(c_ctx).at[1:1 + dec_batch].set(c)
    mod_params = (cond, w_mod, b_mod.reshape(DEPTH, 1, 6 * D_MODEL))
    mods = _modulation(*mod_params, layer=0)

    mixer_w = (attn_sink, w_in, w_pool,
               pool_scale.reshape(DEPTH, 1, POOL_W), w_out,
               ln1_g.reshape(DEPTH, 1, D_MODEL), ln1_b.reshape(DEPTH, 1, D_MODEL))
    conv_b_r = conv_b.reshape(DEPTH, 1, 2 * D_FF)
    ln2_g_r, ln2_b_r = ln2_g.reshape(DEPTH, 1, D_MODEL), ln2_b.reshape(DEPTH, 1, D_MODEL)
    rope = _rope_tables(dec_seq)
    cache_k_r = cache_k.reshape(dec_batch, DEPTH, past, KV_W)
    cache_v_r = cache_v.reshape(dec_batch, DEPTH, past, KV_W)

    xc = x_prompt.reshape(batch * seq, D_MODEL)
    xl = x_sample.reshape(dec_batch * dec_seq, D_MODEL)
    ctx_tiles = batch * seq // TILE
    lat_tiles = dec_batch * dec_seq // TILE
    ctx_row = lambda i: 0
    lat_row = lambda i: i + 1
    ctx_group, lat_group = (ctx_tiles, seq), (lat_tiles, dec_seq)
    ks, vs = [], []
    for l in range(DEPTH):
        mods_l = mods.reshape(MOD_ROWS, 1, 6 * D_MODEL)
        ctx_out = _mixer_call(xc, mods_l, ctx_row, *mixer_w, layer=l, latent=False, seq=seq,
                              n_tiles=ctx_tiles, in_tile0=0,
                              next_mod_params=mod_params if l + 1 < DEPTH else None)
        xc, k_l, v_l = ctx_out[:3]
        mods = ctx_out[3] if l + 1 < DEPTH else None
        xl, w_up_b, w_down_b = _mixer_call(
            xl, mods_l, lat_row, *mixer_w, layer=l, latent=True, seq=dec_seq, n_tiles=lat_tiles,
            in_tile0=0, cache_k=cache_k_r, cache_v=cache_v_r, rope=rope, ffn_f32=(w_up, w_down))
        ffn_w = (w_up_b, conv_w, conv_b_r, w_down_b, ln2_g_r, ln2_b_r)
        xc = _ffn_call([xc], [ctx_group], mods_l, ctx_row, *ffn_w, layer=l, name="ffn_context")
        xl = _ffn_call([xl], [lat_group], mods_l, lat_row, *ffn_w, layer=l, name="ffn_latent")
        ks.append(k_l.reshape(batch, seq, KV_W))
        vs.append(v_l.reshape(batch, seq, KV_W))
    y_prompt = xc.reshape(batch, seq, D_MODEL)
    y_sample = xl.reshape(dec_batch, dec_seq, D_MODEL)
    cache_shape = (batch, DEPTH, seq, N_KV_HEADS, HEAD_DIM)
    return (y_prompt, y_sample, jnp.stack(ks, axis=1).reshape(cache_shape),
            jnp.stack(vs, axis=1).reshape(cache_shape))
```

```python
import functools

import jax
import jax.numpy as jnp
from jax import lax
from jax.experimental import pallas as pl
from jax.experimental.pallas import tpu as pltpu

D_MODEL = 1024
DEPTH = 4
N_HEADS = 8
N_KV_HEADS = 2
HEAD_DIM = 64
GQA_GROUP = N_HEADS // N_KV_HEADS
ATTN_W = N_HEADS * HEAD_DIM
KV_W = N_KV_HEADS * HEAD_DIM
POOL_WINDOWS = (2, 4, 8, 16)
N_POOL_GROUPS = 4
POOL_W = D_MODEL // 2
POOL_GROUP_W = POOL_W // N_POOL_GROUPS
MIX_W = ATTN_W + POOL_W
IN_W = ATTN_W + 2 * KV_W + POOL_W
D_FF = 2816
GRID_W = 64
WINDOW = 128
ROPE_BASE = 10000.0
LN_EPS = 1e-5
DEEPNORM_ALPHA = (2 * DEPTH) ** 0.25
ATTN_SCALE = HEAD_DIM ** -0.5
NEG_INF = -1e30
LOG2E = 1.4426950408889634

LANES = 128
SUBLANES = 8
TILE = 1024
FF_CHUNK = 256
N_FF_CHUNKS = D_FF // FF_CHUNK
EPI_CHUNKS = (256, 256, 256, 256)
EPI_ROWS = max(EPI_CHUNKS)
LATENT_ATTN_UNROLL = 4
CONTEXT_ATTN_UNROLL = 2
MOD_COLS = 1536
MOD_ROWS = 16
VMEM_LIMIT = 60000 * 1024

F32 = jnp.float32
BF16 = jnp.bfloat16


def _silu(x):
    return x / (1.0 + jnp.exp(-x))


def _layer_norm(y, g, b):
    mu = jnp.mean(y, axis=-1, keepdims=True)
    yc = y - mu
    var = jnp.mean(yc * yc, axis=-1, keepdims=True)
    return yc * lax.rsqrt(var + LN_EPS) * g + b


def _chunk_bounds(sizes):
    assert sum(sizes) == TILE
    return [(sum(sizes[:q]), n) for q, n in enumerate(sizes)]


def _mod_slice(mod_ref, j):
    return mod_ref[:, j * D_MODEL:(j + 1) * D_MODEL]


def _mm(a, w):
    return lax.dot_general(a, w, (((1,), (0,)), ((), ())), preferred_element_type=F32)


COL_BLOCKS = D_MODEL // LANES


def _cols_load(ref, r0, n):
    return jnp.concatenate([ref[j, r0:r0 + n, :] for j in range(COL_BLOCKS)], axis=1)


def _cols_store(ref, r0, y):
    for j in range(COL_BLOCKS):
        ref[j, r0:r0 + y.shape[0], :] = y[:, j * LANES:(j + 1) * LANES]


def _store_permuted(ref, y, row0, seq):
    per = seq // SUBLANES
    for q in range(y.shape[0] // per):
        t0 = row0 + q * per
        s, a = t0 // seq, (t0 % seq) // per
        for j in range(COL_BLOCKS):
            ref[j, pl.ds(s * seq + a, per, stride=SUBLANES), :] = (
                y[q * per:(q + 1) * per, j * LANES:(j + 1) * LANES])


def _store_natural(o_ref, stage_ref, y, row0, seq):
    n = y.shape[0]
    per = seq // SUBLANES
    _cols_store(stage_ref, 0, y)
    span = min(n, seq)
    for q0 in range(0, n, span):
        s, b0 = (row0 + q0) // seq, ((row0 + q0) % seq) // SUBLANES
        for a in range(SUBLANES):
            t0 = s * seq + a * per + b0
            for j in range(COL_BLOCKS):
                o_ref[t0:t0 + span // SUBLANES, j * LANES:(j + 1) * LANES] = (
                    stage_ref[j, pl.ds(q0 + a, span // SUBLANES, stride=SUBLANES), :])


def _cols_spec(tile_of=lambda i: i):
    return pl.BlockSpec((COL_BLOCKS, TILE, LANES), lambda i: (0, tile_of(i), 0))


def _whole_spec(shape):
    idx = (0,) * len(shape)
    return pl.BlockSpec(tuple(shape), lambda i: idx, pipeline_mode=pl.Buffered(1))


def _layer_spec(shape, layer):
    idx = (layer,) + (0,) * len(shape)
    return pl.BlockSpec((None,) + tuple(shape), lambda i: idx, pipeline_mode=pl.Buffered(1))


def _mod_kernel(cond_ref, w_ref, b_ref, o_ref):
    o_ref[...] = _mm(_silu(cond_ref[...]).astype(BF16), w_ref[...]) + b_ref[...]


def _mod_specs(layer, n_steps):
    cols = (6 * D_MODEL) // n_steps
    assert cols * n_steps == 6 * D_MODEL and cols % LANES == 0
    in_specs = [pl.BlockSpec((MOD_ROWS, D_MODEL), lambda j: (0, 0)),
                pl.BlockSpec((None, D_MODEL, cols), lambda j: (layer, 0, j)),
                pl.BlockSpec((None, 1, cols), lambda j: (layer, 0, j))]
    return in_specs, pl.BlockSpec((MOD_ROWS, cols), lambda j: (0, j))


def _modulation(cond, w_mod, b_mod, layer):
    n_steps = (6 * D_MODEL) // MOD_COLS
    in_specs, out_spec = _mod_specs(layer, n_steps)
    return pl.pallas_call(
        _mod_kernel,
        grid=(n_steps,),
        in_specs=in_specs,
        out_specs=out_spec,
        out_shape=jax.ShapeDtypeStruct((MOD_ROWS, 6 * D_MODEL), F32),
        compiler_params=pltpu.CompilerParams(
            dimension_semantics=("arbitrary",), vmem_limit_bytes=VMEM_LIMIT),
        name="modulation",
    )(cond, w_mod, b_mod)


def _dup_halves(x):
    lo = lax.broadcasted_iota(jnp.int32, x.shape, 1) < HEAD_DIM
    xr = pltpu.roll(x, HEAD_DIM, axis=1)
    return jnp.where(lo, x, xr), jnp.where(lo, xr, x)


def _attn_scores(lhs, key_sets):
    scores = []
    for k2, _, bias in key_sets:
        s = lax.dot_general(lhs, k2, (((1,), (1,)), ((), ())), preferred_element_type=F32)
        if bias is not None:
            rows, cols = s.shape
            s = (s.reshape(GQA_GROUP, rows // GQA_GROUP, cols) + bias[None]).reshape(rows, cols)
        scores.append(s)
    return scores


def _softmax_pv(scores, sink_col, key_sets):
    def lane_blocks(a):
        return [a[:, j * LANES:(j + 1) * LANES] for j in range(a.shape[1] // LANES)]

    m_lanes = functools.reduce(jnp.maximum, [b for s in scores for b in lane_blocks(s)])
    m = jnp.maximum(jnp.max(m_lanes, axis=-1, keepdims=True), sink_col)
    e_lanes = None
    out = None
    for s, (_, v2, _) in zip(scores, key_sets):
        e = jnp.exp2(s - m)
        e_sum = functools.reduce(jnp.add, lane_blocks(e))
        e_lanes = e_sum if e_lanes is None else e_lanes + e_sum
        o = jnp.dot(e.astype(BF16), v2, preferred_element_type=F32)
        out = o if out is None else out + o
    denom = jnp.sum(e_lanes, axis=-1, keepdims=True) + jnp.exp2(sink_col - m)
    return out / denom


def _mixer_kernel(*refs, latent, seq, layer, next_mods):
    if latent:
        (sink_ref, x_ref, mod_ref, w_in_ref, w_pool_ref, ps_ref, w_out_ref, g_ref, b_ref,
         kc_ref, vc_ref, rope_ref, wu_f32_ref, wd_f32_ref,
         x1_ref, wu_bf16_ref, wd_bf16_ref, q_s, k_s, v_s, mix_s, kc_s, vc_s) = refs
    elif next_mods:
        (sink_ref, x_ref, mod_ref, w_in_ref, w_pool_ref, ps_ref, w_out_ref, g_ref, b_ref,
         cond_ref, wm_ref, bm_ref,
         x1_ref, k_out_ref, v_out_ref, mods_next_ref, q_s, k_s, v_s, mix_s) = refs
    else:
        (sink_ref, x_ref, mod_ref, w_in_ref, w_pool_ref, ps_ref, w_out_ref, g_ref, b_ref,
         x1_ref, k_out_ref, v_out_ref, q_s, k_s, v_s, mix_s) = refs

    h = (x_ref[...] * (1.0 + _mod_slice(mod_ref, 1)) + _mod_slice(mod_ref, 0)).astype(BF16)
    p = _mm(h, w_in_ref[:, ATTN_W + 2 * KV_W:])
    qkv = _mm(h, w_in_ref[:, :ATTN_W + 2 * KV_W])
    if latent:
        wu_bf16_ref[...] = wu_f32_ref[...].astype(BF16)
        wd_bf16_ref[...] = wd_f32_ref[...].astype(BF16)
    elif next_mods:
        _mod_kernel(cond_ref, wm_ref, bm_ref, mods_next_ref)
    k = qkv[:, ATTN_W:ATTN_W + KV_W]
    v = qkv[:, ATTN_W + KV_W:ATTN_W + 2 * KV_W]

    n_seq = TILE // seq
    sub = lax.broadcasted_iota(jnp.int32, (SUBLANES, POOL_GROUP_W), 0)

    def edge_rows(a, first_fn, last_fn):
        pieces = []
        for s in range(n_seq):
            blk = a[s * seq:(s + 1) * seq]
            head = blk[:SUBLANES] if first_fn is None else first_fn(blk[:SUBLANES])
            tail = blk[seq - SUBLANES:] if last_fn is None else last_fn(blk[seq - SUBLANES:])
            pieces += [head, blk[SUBLANES:seq - SUBLANES], tail]
        return jnp.concatenate(pieces, axis=0)

    def shift(a, j):
        rolled = pltpu.roll(a, (-j) % TILE, axis=0)
        if j < 0:
            return edge_rows(rolled, lambda r: jnp.where(sub >= -j, r, 0.0), None)
        return edge_rows(rolled, None, lambda r: jnp.where(sub < SUBLANES - j, r, 0.0))

    def pool_group(g):
        pg = p[:, g * POOL_GROUP_W:(g + 1) * POOL_GROUP_W]
        half = POOL_WINDOWS[g] // 2
        left, right, n = shift(pg, -1), pg, 1
        while n < half:
            left = left + shift(left, -n)
            right = right + shift(right, n)
            n *= 2
        total = left + right
        first_cnt = jnp.minimum(sub + half, 2 * half).astype(F32)
        last_cnt = jnp.minimum(SUBLANES - sub + half, 2 * half).astype(F32)
        mean = edge_rows(total * (1.0 / (2 * half)),
                         lambda r: r * (2.0 * half) / first_cnt,
                         lambda r: r * (2.0 * half) / last_cnt)
        d = (mean - pg).astype(BF16)
        y = _mm(d, w_pool_ref[g])
        y = y * ps_ref[:, g * POOL_GROUP_W:(g + 1) * POOL_GROUP_W]
        mix_s[:, ATTN_W + g * POOL_GROUP_W:ATTN_W + (g + 1) * POOL_GROUP_W] = y.astype(BF16)

    if latent:
        cos = rope_ref[0]
        sin_up = rope_ref[1]
        sin_dn = rope_ref[2]

        def rope(t):
            return (t * cos + pltpu.roll(t, LANES - HEAD_DIM // 4, axis=1) * sin_up
                    + pltpu.roll(t, HEAD_DIM // 4, axis=1) * sin_dn)

        k = rope(k)
        pad = WINDOW
        zeros = jnp.zeros((pad, LANES), BF16)
        for hh in range(N_KV_HEADS):
            k_s[hh, 0:pad, :] = zeros
            k_s[hh, pad + TILE:2 * pad + TILE, :] = zeros
            v_s[hh, 0:pad, :] = zeros
            v_s[hh, pad + TILE:2 * pad + TILE, :] = zeros
        kc0, kc1 = _dup_halves(kc_ref[...])
        vc0, vc1 = _dup_halves(vc_ref[...])
        kc_s[0] = kc0.astype(BF16)
        kc_s[1] = kc1.astype(BF16)
        vc_s[0] = vc0.astype(BF16)
        vc_s[1] = vc1.astype(BF16)
    else:
        pad = 0
        k_out_ref[...] = k
        v_out_ref[...] = v

    for j in range(ATTN_W // LANES):
        pool_group(j)
        qj = qkv[:, j * LANES:(j + 1) * LANES]
        if latent:
            qj = rope(qj)
        q_s[:, j * LANES:(j + 1) * LANES] = (qj * (ATTN_SCALE * LOG2E)).astype(BF16)
    k0, k1 = _dup_halves(k)
    v0, v1 = _dup_halves(v)
    k_s[0, pad:pad + TILE, :] = k0.astype(BF16)
    k_s[1, pad:pad + TILE, :] = k1.astype(BF16)
    v_s[0, pad:pad + TILE, :] = v0.astype(BF16)
    v_s[1, pad:pad + TILE, :] = v1.astype(BF16)

    qb = WINDOW if latent else seq
    n_rows = GQA_GROUP * qb
    row = lax.broadcasted_iota(jnp.int32, (n_rows, 1), 0)
    lo_q = lax.broadcasted_iota(jnp.int32, (qb, LANES), 1) < HEAD_DIM
    if latent:
        r = lax.broadcasted_iota(jnp.int32, (qb, 3 * WINDOW), 0)
        c = lax.broadcasted_iota(jnp.int32, (qb, 3 * WINDOW), 1)
        in_band = jnp.abs(r - c + WINDOW) <= WINDOW

    def key_sets_of(i, hh):
        r0 = pl.multiple_of(i * qb, qb)
        if not latent:
            return [(k_s[hh, pl.ds(r0, qb), :], v_s[hh, pl.ds(r0, qb), :], None)]
        kpos = (i - 1) * WINDOW + c
        valid = in_band & (kpos >= 0) & (kpos < TILE)
        bias = jnp.where(valid, 0.0, NEG_INF * LOG2E).astype(F32)
        return [(kc_s[hh], vc_s[hh], None),
                (k_s[hh, pl.ds(r0, 3 * WINDOW), :], v_s[hh, pl.ds(r0, 3 * WINDOW), :], bias)]

    def scores_of(i, hh):
        r0 = pl.multiple_of(i * qb, qb)
        parts = []
        for pair in range(GQA_GROUP // 2):
            c0 = hh * GQA_GROUP * HEAD_DIM + pair * LANES
            qp = q_s[pl.ds(r0, qb), c0:c0 + LANES]
            parts.append(jnp.where(lo_q, qp, jnp.zeros_like(qp)))
            parts.append(jnp.where(lo_q, jnp.zeros_like(qp), qp))
        lhs = jnp.concatenate(parts, axis=0)
        return _attn_scores(lhs, key_sets_of(i, hh))

    def finish(i, hh, scores):
        r0 = pl.multiple_of(i * qb, qb)
        h0 = hh * GQA_GROUP
        sink_col = jnp.full((n_rows, 1), sink_ref[layer, h0 + GQA_GROUP - 1] * LOG2E, F32)
        for g in range(GQA_GROUP - 2, -1, -1):
            sink_col = jnp.where(row < (g + 1) * qb, sink_ref[layer, h0 + g] * LOG2E, sink_col)
        o = _softmax_pv(scores, sink_col, key_sets_of(i, hh))
        for pair in range(GQA_GROUP // 2):
            c0 = hh * GQA_GROUP * HEAD_DIM + pair * LANES
            oa = o[(2 * pair) * qb:(2 * pair + 1) * qb]
            ob = o[(2 * pair + 1) * qb:(2 * pair + 2) * qb]
            mix_s[pl.ds(r0, qb), c0:c0 + LANES] = jnp.where(lo_q, oa, ob).astype(BF16)

    unroll = LATENT_ATTN_UNROLL if latent else CONTEXT_ATTN_UNROLL

    def attend_blocks(it, carry):
        groups = [(it * unroll + u, hh) for u in range(unroll) for hh in range(N_KV_HEADS)]
        if latent:
            scores = scores_of(*groups[0])
            for g, group in enumerate(groups):
                cur = scores
                if g + 1 < len(groups):
                    scores = scores_of(*groups[g + 1])
                finish(*group, cur)
        else:
            for group in groups:
                finish(*group, scores_of(*group))
        return carry

    lax.fori_loop(0, TILE // (qb * unroll), attend_blocks, 0)

    def out_project(r0, n):
        return _mm(mix_s[r0:r0 + n, :], w_out_ref[...])

    bounds = _chunk_bounds(EPI_CHUNKS)
    mix_next = out_project(*bounds[0])
    for q, (r0, n) in enumerate(bounds):
        mix = mix_next
        if q + 1 < len(bounds):
            mix_next = out_project(*bounds[q + 1])
        y = DEEPNORM_ALPHA * x_ref[r0:r0 + n, :] + _mod_slice(mod_ref, 2) * mix
        _store_permuted(x1_ref, _layer_norm(y, g_ref[...], b_ref[...]), r0, seq)


def _mixer_call(x, mods, mod_row, sink, w_in, w_pool, pool_scale, w_out, ln_g, ln_b, *,
                layer, latent, seq, n_tiles, in_tile0, cache_k=None, cache_v=None, rope=None,
                ffn_f32=(), next_mod_params=None):
    n_tok = n_tiles * TILE
    in_specs = [
        pl.BlockSpec(memory_space=pltpu.SMEM),
        pl.BlockSpec((TILE, D_MODEL), lambda i: (i + in_tile0, 0)),
        pl.BlockSpec((None, 1, 6 * D_MODEL), lambda i: (mod_row(i), 0, 0)),
        _layer_spec((D_MODEL, IN_W), layer),
        _layer_spec((N_POOL_GROUPS, POOL_GROUP_W, POOL_GROUP_W), layer),
        _layer_spec((1, POOL_W), layer),
        _layer_spec((MIX_W, D_MODEL), layer),
        _layer_spec((1, D_MODEL), layer),
        _layer_spec((1, D_MODEL), layer),
    ]
    args = [sink, x, mods, w_in, w_pool, pool_scale, w_out, ln_g, ln_b]
    pad = WINDOW if latent else 0
    scratch = [
        pltpu.VMEM((TILE, ATTN_W), BF16),
        pltpu.VMEM((N_KV_HEADS, TILE + 2 * pad, LANES), BF16),
        pltpu.VMEM((N_KV_HEADS, TILE + 2 * pad, LANES), BF16),
        pltpu.VMEM((TILE, MIX_W), BF16),
    ]
    x_out = jax.ShapeDtypeStruct((COL_BLOCKS, n_tok, LANES), F32)
    x_spec = _cols_spec()
    if latent:
        past = cache_k.shape[2]
        in_specs += [
            pl.BlockSpec((None, None, past, KV_W), lambda i: (i, layer, 0, 0)),
            pl.BlockSpec((None, None, past, KV_W), lambda i: (i, layer, 0, 0)),
            pl.BlockSpec((3, TILE, LANES), lambda i: (0, 0, 0), pipeline_mode=pl.Buffered(1)),
        ]
        args += [cache_k, cache_v, rope]
        scratch += [pltpu.VMEM((N_KV_HEADS, past, LANES), BF16),
                    pltpu.VMEM((N_KV_HEADS, past, LANES), BF16)]
        out_shape, out_specs = [x_out], [x_spec]
        for w in ffn_f32:
            rows = w.shape[1] // n_tiles
            assert rows * n_tiles == w.shape[1] and rows % 16 == 0
            in_specs.append(pl.BlockSpec((None, rows, w.shape[2]), lambda i: (layer, i, 0)))
            args.append(w)
            out_shape.append(jax.ShapeDtypeStruct(w.shape[1:], BF16))
            out_specs.append(pl.BlockSpec((rows, w.shape[2]), lambda i: (i, 0)))
    else:
        kv_out = jax.ShapeDtypeStruct((n_tok, KV_W), F32)
        kv_spec = pl.BlockSpec((TILE, KV_W), lambda i: (i, 0))
        out_shape = [x_out, kv_out, kv_out]
        out_specs = [x_spec, kv_spec, kv_spec]
        if next_mod_params is not None:
            mod_in_specs, mod_out_spec = _mod_specs(layer + 1, n_tiles)
            in_specs += mod_in_specs
            args += list(next_mod_params)
            out_shape.append(jax.ShapeDtypeStruct((MOD_ROWS, 6 * D_MODEL), F32))
            out_specs.append(mod_out_spec)
    return pl.pallas_call(
        functools.partial(_mixer_kernel, latent=latent, seq=seq, layer=layer,
                          next_mods=next_mod_params is not None),
        grid=(n_tiles,),
        in_specs=in_specs,
        out_specs=out_specs,
        out_shape=out_shape,
        scratch_shapes=scratch,
        compiler_params=pltpu.CompilerParams(
            dimension_semantics=("arbitrary",), vmem_limit_bytes=VMEM_LIMIT),
        name="mixer_latent" if latent else "mixer_context",
    )(*args)


def _ffn_kernel(*refs, groups):
    x_refs, rest = refs[:len(groups)], refs[len(groups):]
    step = pl.program_id(0)
    tile0 = 0
    for x_ref, (n_tiles, seq) in zip(x_refs, groups):
        tile_fn = functools.partial(_ffn_tile, x_ref, *rest, seq=seq)
        if len(groups) == 1:
            tile_fn()
        else:
            pl.when((step >= tile0) & (step < tile0 + n_tiles))(tile_fn)
        tile0 += n_tiles


def _ffn_tile(x_ref, mod_ref, w_up_ref, cw_ref, cb_ref, w_down_ref, g_ref, b_ref, o_ref,
              h_s, u_s, act_s, stage_s, *, seq):
    n_seq = TILE // seq
    sub = lax.broadcasted_iota(jnp.int32, (SUBLANES, FF_CHUNK), 0)

    h_s[...] = (_cols_load(x_ref, 0, TILE) * (1.0 + _mod_slice(mod_ref, 4))
                + _mod_slice(mod_ref, 3)).astype(BF16)

    def up_project(c):
        hb = h_s[...]
        for part in range(2):
            col = part * D_FF + c * FF_CHUNK
            u_s[c % 2, part] = jnp.dot(hb, w_up_ref[:, col:col + FF_CHUNK],
                                       preferred_element_type=F32)

    def conv_swiglu(c):
        for s in range(n_seq):
            halves = []
            for part in range(2):
                col = part * D_FF + c * FF_CHUNK
                cw = cw_ref[:, col:col + FF_CHUNK]
                cur = u_s[c % 2, part, s * seq:(s + 1) * seq, :]
                wrap_prev = jnp.where(sub == 0, 0.0, pltpu.roll(cur[seq - SUBLANES:], 1, axis=0))
                wrap_next = jnp.where(sub == SUBLANES - 1, 0.0,
                                      pltpu.roll(cur[:SUBLANES], SUBLANES - 1, axis=0))
                prev = jnp.concatenate([wrap_prev, cur[:seq - SUBLANES]], axis=0)
                nxt = jnp.concatenate([cur[SUBLANES:], wrap_next], axis=0)
                halves.append(prev * cw[0:1] + cur * cw[1:2] + nxt * cw[2:3]
                              + cb_ref[:, col:col + FF_CHUNK])
            a, g = halves
            act_s[s * seq:(s + 1) * seq, c * FF_CHUNK:(c + 1) * FF_CHUNK] = (_silu(g) * a).astype(BF16)

    up_project(0)
    for c in range(N_FF_CHUNKS):
        if c + 1 < N_FF_CHUNKS:
            up_project(c + 1)
        conv_swiglu(c)

    def down_project(r0, n):
        return _mm(act_s[r0:r0 + n, :], w_down_ref[...])

    bounds = _chunk_bounds(EPI_CHUNKS)
    ff_next = down_project(*bounds[0])
    for q, (r0, n) in enumerate(bounds):
        ff = ff_next
        if q + 1 < len(bounds):
            ff_next = down_project(*bounds[q + 1])
        y = DEEPNORM_ALPHA * _cols_load(x_ref, r0, n) + _mod_slice(mod_ref, 5) * ff
        _store_natural(o_ref, stage_s, _layer_norm(y, g_ref[...], b_ref[...]), r0, seq)


def _ffn_call(xs, groups, mods, mod_row, w_up, conv_w, conv_b, w_down, ln_g, ln_b, *, layer,
              name):
    n_steps = sum(n for n, _ in groups)
    scratch = [pltpu.VMEM((TILE, D_MODEL), BF16),
               pltpu.VMEM((2, 2, TILE, FF_CHUNK), F32),
               pltpu.VMEM((TILE, D_FF), BF16),
               pltpu.VMEM((COL_BLOCKS, EPI_ROWS, LANES), F32)]
    out_specs = pl.BlockSpec((TILE, D_MODEL), lambda i: (i, 0))
    out_shape = jax.ShapeDtypeStruct((n_steps * TILE, D_MODEL), F32)
    x_specs, tile0 = [], 0
    for n_tiles, _ in groups:
        x_specs.append(_cols_spec(
            lambda i, tile0=tile0, n_tiles=n_tiles: jnp.clip(i - tile0, 0, n_tiles - 1)))
        tile0 += n_tiles
    return pl.pallas_call(
        functools.partial(_ffn_kernel, groups=tuple(groups)),
        grid=(n_steps,),
        in_specs=x_specs + [
            pl.BlockSpec((None, 1, 6 * D_MODEL), lambda i: (mod_row(i), 0, 0)),
            _whole_spec((D_MODEL, 2 * D_FF)),
            _layer_spec((3, 2 * D_FF), layer),
            _layer_spec((1, 2 * D_FF), layer),
            _whole_spec((D_FF, D_MODEL)),
            _layer_spec((1, D_MODEL), layer),
            _layer_spec((1, D_MODEL), layer),
        ],
        out_specs=out_specs,
        out_shape=out_shape,
        scratch_shapes=scratch,
        compiler_params=pltpu.CompilerParams(
            dimension_semantics=("arbitrary",), vmem_limit_bytes=VMEM_LIMIT),
        name=name,
    )(*xs, mods, w_up, conv_w, conv_b, w_down, ln_g, ln_b)


def _rope_tables(n_pos):
    half = HEAD_DIM // 2
    t = jnp.arange(n_pos)
    inv_freq = ROPE_BASE ** (-jnp.arange(0, half, 2, dtype=F32) / half)

    def ang(p):
        a = p.astype(F32)[:, None] * inv_freq[None, :]
        return jnp.concatenate([a, a], axis=-1)

    a = jnp.concatenate([ang(t // GRID_W), ang(t % GRID_W)], axis=-1)
    cos = jnp.cos(a)
    sin = jnp.sin(a)
    first = (jnp.arange(HEAD_DIM) % half) < half // 2
    sin_up = jnp.where(first[None, :], -sin, 0.0)
    sin_dn = jnp.where(first[None, :], 0.0, sin)
    tabs = jnp.stack([cos, sin_up, sin_dn])
    return jnp.concatenate([tabs, tabs], axis=-1).astype(F32)


def kernel(x_prompt, x_sample, cache_k, cache_v, c, c_ctx, w_mod, b_mod, w_in, attn_sink,
           w_pool, pool_scale, w_out, ln1_g, ln1_b, w_up, conv_w, conv_b, w_down, ln2_g, ln2_b):
    batch, seq, _ = x_prompt.shape
    dec_batch, dec_seq, _ = x_sample.shape
    past = cache_k.shape[2]
    assert TILE % seq == 0 and dec_seq == TILE and seq & (seq - 1) == 0
    assert (batch * seq) % TILE == 0 and past % 16 == 0

    cond = jnp.zeros((MOD_ROWS, D_MODEL), F32).at[0].set(c_ctx).at[1:1 + dec_batch].set(c)
    mod_params = (cond, w_mod, b_mod.reshape(DEPTH, 1, 6 * D_MODEL))
    mods = _modulation(*mod_params, layer=0)

    mixer_w = (attn_sink, w_in, w_pool,
               pool_scale.reshape(DEPTH, 1, POOL_W), w_out,
               ln1_g.reshape(DEPTH, 1, D_MODEL), ln1_b.reshape(DEPTH, 1, D_MODEL))
    conv_b_r = conv_b.reshape(DEPTH, 1, 2 * D_FF)
    ln2_g_r, ln2_b_r = ln2_g.reshape(DEPTH, 1, D_MODEL), ln2_b.reshape(DEPTH, 1, D_MODEL)
    rope = _rope_tables(dec_seq)
    cache_k_r = cache_k.reshape(dec_batch, DEPTH, past, KV_W)
    cache_v_r = cache_v.reshape(dec_batch, DEPTH, past, KV_W)

    xc = x_prompt.reshape(batch * seq, D_MODEL)
    xl = x_sample.reshape(dec_batch * dec_seq, D_MODEL)
    ctx_tiles = batch * seq // TILE
    lat_tiles = dec_batch * dec_seq // TILE
    ctx_row = lambda i: 0
    lat_row = lambda i: i + 1
    ctx_group, lat_group = (ctx_tiles, seq), (lat_tiles, dec_seq)
    ks, vs = [], []
    for l in range(DEPTH):
        mods_l = mods.reshape(MOD_ROWS, 1, 6 * D_MODEL)
        ctx_out = _mixer_call(xc, mods_l, ctx_row, *mixer_w, layer=l, latent=False, seq=seq,
                              n_tiles=ctx_tiles, in_tile0=0,
                              next_mod_params=mod_params if l + 1 < DEPTH else None)
        xc, k_l, v_l = ctx_out[:3]
        mods = ctx_out[3] if l + 1 < DEPTH else None
        xl, w_up_b, w_down_b = _mixer_call(
            xl, mods_l, lat_row, *mixer_w, layer=l, latent=True, seq=dec_seq, n_tiles=lat_tiles,
            in_tile0=0, cache_k=cache_k_r, cache_v=cache_v_r, rope=rope, ffn_f32=(w_up, w_down))
        ffn_w = (w_up_b, conv_w, conv_b_r, w_down_b, ln2_g_r, ln2_b_r)
        xc = _ffn_call([xc], [ctx_group], mods_l, ctx_row, *ffn_w, layer=l, name="ffn_context")
        xl = _ffn_call([xl], [lat_group], mods_l, lat_row, *ffn_w, layer=l, name="ffn_latent")
        ks.append(k_l.reshape(batch, seq, KV_W))
        vs.append(v_l.reshape(batch, seq, KV_W))
    y_prompt = xc.reshape(batch, seq, D_MODEL)
    y_sample = xl.reshape(dec_batch, dec_seq, D_MODEL)
    cache_shape = (batch, DEPTH, seq, N_KV_HEADS, HEAD_DIM)
    return (y_prompt, y_sample, jnp.stack(ks, axis=1).reshape(cache_shape),
            jnp.stack(vs, axis=1).reshape(cache_shape))
```

```python
import functools

import jax
import jax.numpy as jnp
from jax import lax
from jax.experimental import pallas as pl
from jax.experimental.pallas import tpu as pltpu

D_MODEL = 1024
DEPTH = 4
N_HEADS = 8
N_KV_HEADS = 2
HEAD_DIM = 64
GQA_GROUP = N_HEADS // N_KV_HEADS
ATTN_W = N_HEADS * HEAD_DIM
KV_W = N_KV_HEADS * HEAD_DIM
POOL_WINDOWS = (2, 4, 8, 16)
N_POOL_GROUPS = 4
POOL_W = D_MODEL // 2
POOL_GROUP_W = POOL_W // N_POOL_GROUPS
MIX_W = ATTN_W + POOL_W
IN_W = ATTN_W + 2 * KV_W + POOL_W
D_FF = 2816
GRID_W = 64
WINDOW = 128
ROPE_BASE = 10000.0
LN_EPS = 1e-5
DEEPNORM_ALPHA = (2 * DEPTH) ** 0.25
ATTN_SCALE = HEAD_DIM ** -0.5
NEG_INF = -1e30
LOG2E = 1.4426950408889634

LANES = 128
SUBLANES = 8
TILE = 1024
FF_CHUNK = 256
N_FF_CHUNKS = D_FF // FF_CHUNK
EPI_CHUNKS = (256, 256, 256, 256)
EPI_ROWS = max(EPI_CHUNKS)
LATENT_ATTN_UNROLL = 4
CONTEXT_ATTN_UNROLL = 2
MOD_COLS = 1536
MOD_ROWS = 16
VMEM_LIMIT = 60000 * 1024

F32 = jnp.float32
BF16 = jnp.bfloat16


def _silu(x):
    return x / (1.0 + jnp.exp(-x))


def _layer_norm(y, g, b):
    mu = jnp.mean(y, axis=-1, keepdims=True)
    yc = y - mu
    var = jnp.mean(yc * yc, axis=-1, keepdims=True)
    return yc * lax.rsqrt(var + LN_EPS) * g + b


def _chunk_bounds(sizes):
    assert sum(sizes) == TILE
    return [(sum(sizes[:q]), n) for q, n in enumerate(sizes)]


def _mod_slice(mod_ref, row, j):
    return mod_ref[pl.ds(row, 1), j * D_MODEL:(j + 1) * D_MODEL]


def _layer_row(ref, layer, c0=0, n=None):
    return ref[layer:layer + 1, c0:(ref.shape[1] if n is None else c0 + n)]


def _mm(a, w):
    return lax.dot_general(a, w, (((1,), (0,)), ((), ())), preferred_element_type=F32)


COL_BLOCKS = D_MODEL // LANES


def _cols_load(ref, r0, n):
    return jnp.concatenate([ref[j, r0:r0 + n, :] for j in range(COL_BLOCKS)], axis=1)


def _cols_store(ref, r0, y):
    for j in range(COL_BLOCKS):
        ref[j, r0:r0 + y.shape[0], :] = y[:, j * LANES:(j + 1) * LANES]


def _store_permuted(ref, y, row0, seq):
    per = seq // SUBLANES
    for q in range(y.shape[0] // per):
        t0 = row0 + q * per
        s, a = t0 // seq, (t0 % seq) // per
        for j in range(COL_BLOCKS):
            ref[j, pl.ds(s * seq + a, per, stride=SUBLANES), :] = (
                y[q * per:(q + 1) * per, j * LANES:(j + 1) * LANES])


def _store_natural(o_ref, stage_ref, y, row0, seq):
    n = y.shape[0]
    per = seq // SUBLANES
    _cols_store(stage_ref, 0, y)
    span = min(n, seq)
    for q0 in range(0, n, span):
        s, b0 = (row0 + q0) // seq, ((row0 + q0) % seq) // SUBLANES
        for a in range(SUBLANES):
            t0 = s * seq + a * per + b0
            for j in range(COL_BLOCKS):
                o_ref[t0:t0 + span // SUBLANES, j * LANES:(j + 1) * LANES] = (
                    stage_ref[j, pl.ds(q0 + a, span // SUBLANES, stride=SUBLANES), :])


def _cols_spec(tile_of=lambda i: i):
    return pl.BlockSpec((COL_BLOCKS, TILE, LANES), lambda i: (0, tile_of(i), 0))


def _whole_spec(shape):
    idx = (0,) * len(shape)
    return pl.BlockSpec(tuple(shape), lambda i: idx, pipeline_mode=pl.Buffered(1))


def _layer_spec(shape, layer):
    idx = (layer,) + (0,) * len(shape)
    return pl.BlockSpec((None,) + tuple(shape), lambda i: idx, pipeline_mode=pl.Buffered(1))


def _mod_kernel(cond_ref, w_ref, b_ref, o_ref, *, layer):
    o_ref[...] = (_mm(_silu(cond_ref[...]).astype(BF16), w_ref[...])
                  + _layer_row(b_ref, layer))


def _mod_specs(layer, n_steps):
    cols = (6 * D_MODEL) // n_steps
    assert cols * n_steps == 6 * D_MODEL and cols % LANES == 0
    in_specs = [pl.BlockSpec((MOD_ROWS, D_MODEL), lambda j: (0, 0)),
                pl.BlockSpec((None, D_MODEL, cols), lambda j: (layer, 0, j)),
                pl.BlockSpec((DEPTH, cols), lambda j: (0, j))]
    return in_specs, pl.BlockSpec((MOD_ROWS, cols), lambda j: (0, j))


def _modulation(cond, w_mod, b_mod, layer):
    n_steps = (6 * D_MODEL) // MOD_COLS
    in_specs, out_spec = _mod_specs(layer, n_steps)
    return pl.pallas_call(
        functools.partial(_mod_kernel, layer=layer),
        grid=(n_steps,),
        in_specs=in_specs,
        out_specs=out_spec,
        out_shape=jax.ShapeDtypeStruct((MOD_ROWS, 6 * D_MODEL), F32),
        compiler_params=pltpu.CompilerParams(
            dimension_semantics=("arbitrary",), vmem_limit_bytes=VMEM_LIMIT),
        name="modulation",
    )(cond, w_mod, b_mod)


def _dup_halves(x):
    lo = lax.broadcasted_iota(jnp.int32, x.shape, 1) < HEAD_DIM
    xr = pltpu.roll(x, HEAD_DIM, axis=1)
    return jnp.where(lo, x, xr), jnp.where(lo, xr, x)


def _attn_scores(lhs, key_sets):
    scores = []
    for k2, _, bias in key_sets:
        s = lax.dot_general(lhs, k2, (((1,), (1,)), ((), ())), preferred_element_type=F32)
        if bias is not None:
            rows, cols = s.shape
            s = (s.reshape(GQA_GROUP, rows // GQA_GROUP, cols) + bias[None]).reshape(rows, cols)
        scores.append(s)
    return scores


def _softmax_pv(scores, sink_col, key_sets):
    def lane_blocks(a):
        return [a[:, j * LANES:(j + 1) * LANES] for j in range(a.shape[1] // LANES)]

    m_lanes = functools.reduce(jnp.maximum, [b for s in scores for b in lane_blocks(s)])
    m = jnp.maximum(jnp.max(m_lanes, axis=-1, keepdims=True), sink_col)
    e_lanes = None
    out = None
    for s, (_, v2, _) in zip(scores, key_sets):
        e = jnp.exp2(s - m)
        e_sum = functools.reduce(jnp.add, lane_blocks(e))
        e_lanes = e_sum if e_lanes is None else e_lanes + e_sum
        o = jnp.dot(e.astype(BF16), v2, preferred_element_type=F32)
        out = o if out is None else out + o
    denom = jnp.sum(e_lanes, axis=-1, keepdims=True) + jnp.exp2(sink_col - m)
    return out / denom


def _mixer_kernel(*refs, latent, seq, layer, next_mods, mod_row):
    if latent:
        (sink_ref, x_ref, mod_ref, w_in_ref, w_pool_ref, ps_ref, w_out_ref, g_ref, b_ref,
         kc_ref, vc_ref, rope_ref, wu_f32_ref, wd_f32_ref,
         x1_ref, wu_bf16_ref, wd_bf16_ref, q_s, k_s, v_s, mix_s, kc_s, vc_s) = refs
    elif next_mods:
        (sink_ref, x_ref, mod_ref, w_in_ref, w_pool_ref, ps_ref, w_out_ref, g_ref, b_ref,
         cond_ref, wm_ref, bm_ref,
         x1_ref, k_out_ref, v_out_ref, mods_next_ref, q_s, k_s, v_s, mix_s) = refs
    else:
        (sink_ref, x_ref, mod_ref, w_in_ref, w_pool_ref, ps_ref, w_out_ref, g_ref, b_ref,
         x1_ref, k_out_ref, v_out_ref, q_s, k_s, v_s, mix_s) = refs

    row_m = mod_row(pl.program_id(0))
    h = (x_ref[...] * (1.0 + _mod_slice(mod_ref, row_m, 1))
         + _mod_slice(mod_ref, row_m, 0)).astype(BF16)
    p = _mm(h, w_in_ref[:, ATTN_W + 2 * KV_W:])
    qkv = _mm(h, w_in_ref[:, :ATTN_W + 2 * KV_W])
    if latent:
        wu_bf16_ref[...] = wu_f32_ref[...].astype(BF16)
        wd_bf16_ref[...] = wd_f32_ref[...].astype(BF16)
    elif next_mods:
        _mod_kernel(cond_ref, wm_ref, bm_ref, mods_next_ref, layer=layer + 1)
    k = qkv[:, ATTN_W:ATTN_W + KV_W]
    v = qkv[:, ATTN_W + KV_W:ATTN_W + 2 * KV_W]

    n_seq = TILE // seq
    sub = lax.broadcasted_iota(jnp.int32, (SUBLANES, POOL_GROUP_W), 0)

    def edge_rows(a, first_fn, last_fn):
        pieces = []
        for s in range(n_seq):
            blk = a[s * seq:(s + 1) * seq]
            head = blk[:SUBLANES] if first_fn is None else first_fn(blk[:SUBLANES])
            tail = blk[seq - SUBLANES:] if last_fn is None else last_fn(blk[seq - SUBLANES:])
            pieces += [head, blk[SUBLANES:seq - SUBLANES], tail]
        return jnp.concatenate(pieces, axis=0)

    def shift(a, j):
        rolled = pltpu.roll(a, (-j) % TILE, axis=0)
        if j < 0:
            return edge_rows(rolled, lambda r: jnp.where(sub >= -j, r, 0.0), None)
        return edge_rows(rolled, None, lambda r: jnp.where(sub < SUBLANES - j, r, 0.0))

    def pool_group(g):
        pg = p[:, g * POOL_GROUP_W:(g + 1) * POOL_GROUP_W]
        half = POOL_WINDOWS[g] // 2
        left, right, n = shift(pg, -1), pg, 1
        while n < half:
            left = left + shift(left, -n)
            right = right + shift(right, n)
            n *= 2
        total = left + right
        first_cnt = jnp.minimum(sub + half, 2 * half).astype(F32)
        last_cnt = jnp.minimum(SUBLANES - sub + half, 2 * half).astype(F32)
        mean = edge_rows(total * (1.0 / (2 * half)),
                         lambda r: r * (2.0 * half) / first_cnt,
                         lambda r: r * (2.0 * half) / last_cnt)
        d = (mean - pg).astype(BF16)
        y = _mm(d, w_pool_ref[g])
        y = y * _layer_row(ps_ref, layer, g * POOL_GROUP_W, POOL_GROUP_W)
        mix_s[:, ATTN_W + g * POOL_GROUP_W:ATTN_W + (g + 1) * POOL_GROUP_W] = y.astype(BF16)

    if latent:
        cos = rope_ref[0]
        sin_up = rope_ref[1]
        sin_dn = rope_ref[2]

        def rope(t):
            return (t * cos + pltpu.roll(t, LANES - HEAD_DIM // 4, axis=1) * sin_up
                    + pltpu.roll(t, HEAD_DIM // 4, axis=1) * sin_dn)

        k = rope(k)
        pad = WINDOW
        zeros = jnp.zeros((pad, LANES), BF16)
        for hh in range(N_KV_HEADS):
            k_s[hh, 0:pad, :] = zeros
            k_s[hh, pad + TILE:2 * pad + TILE, :] = zeros
            v_s[hh, 0:pad, :] = zeros
            v_s[hh, pad + TILE:2 * pad + TILE, :] = zeros
        kc0, kc1 = _dup_halves(kc_ref[...])
        vc0, vc1 = _dup_halves(vc_ref[...])
        kc_s[0] = kc0.astype(BF16)
        kc_s[1] = kc1.astype(BF16)
        vc_s[0] = vc0.astype(BF16)
        vc_s[1] = vc1.astype(BF16)
    else:
        pad = 0
        k_out_ref[...] = k
        v_out_ref[...] = v

    for j in range(ATTN_W // LANES):
        pool_group(j)
        qj = qkv[:, j * LANES:(j + 1) * LANES]
        if latent:
            qj = rope(qj)
        q_s[:, j * LANES:(j + 1) * LANES] = (qj * (ATTN_SCALE * LOG2E)).astype(BF16)
    k0, k1 = _dup_halves(k)
    v0, v1 = _dup_halves(v)
    k_s[0, pad:pad + TILE, :] = k0.astype(BF16)
    k_s[1, pad:pad + TILE, :] = k1.astype(BF16)
    v_s[0, pad:pad + TILE, :] = v0.astype(BF16)
    v_s[1, pad:pad + TILE, :] = v1.astype(BF16)

    qb = WINDOW if latent else seq
    n_rows = GQA_GROUP * qb
    row = lax.broadcasted_iota(jnp.int32, (n_rows, 1), 0)
    lo_q = lax.broadcasted_iota(jnp.int32, (qb, LANES), 1) < HEAD_DIM
    if latent:
        r = lax.broadcasted_iota(jnp.int32, (qb, 3 * WINDOW), 0)
        c = lax.broadcasted_iota(jnp.int32, (qb, 3 * WINDOW), 1)
        in_band = jnp.abs(r - c + WINDOW) <= WINDOW

    def key_sets_of(i, hh):
        r0 = pl.multiple_of(i * qb, qb)
        if not latent:
            return [(k_s[hh, pl.ds(r0, qb), :], v_s[hh, pl.ds(r0, qb), :], None)]
        kpos = (i - 1) * WINDOW + c
        valid = in_band & (kpos >= 0) & (kpos < TILE)
        bias = jnp.where(valid, 0.0, NEG_INF * LOG2E).astype(F32)
        return [(kc_s[hh], vc_s[hh], None),
                (k_s[hh, pl.ds(r0, 3 * WINDOW), :], v_s[hh, pl.ds(r0, 3 * WINDOW), :], bias)]

    def scores_of(i, hh):
        r0 = pl.multiple_of(i * qb, qb)
        parts = []
        for pair in range(GQA_GROUP // 2):
            c0 = hh * GQA_GROUP * HEAD_DIM + pair * LANES
            qp = q_s[pl.ds(r0, qb), c0:c0 + LANES]
            parts.append(jnp.where(lo_q, qp, jnp.zeros_like(qp)))
            parts.append(jnp.where(lo_q, jnp.zeros_like(qp), qp))
        lhs = jnp.concatenate(parts, axis=0)
        return _attn_scores(lhs, key_sets_of(i, hh))

    def finish(i, hh, scores):
        r0 = pl.multiple_of(i * qb, qb)
        h0 = hh * GQA_GROUP
        sink_col = jnp.full((n_rows, 1), sink_ref[layer, h0 + GQA_GROUP - 1] * LOG2E, F32)
        for g in range(GQA_GROUP - 2, -1, -1):
            sink_col = jnp.where(row < (g + 1) * qb, sink_ref[layer, h0 + g] * LOG2E, sink_col)
        o = _softmax_pv(scores, sink_col, key_sets_of(i, hh))
        for pair in range(GQA_GROUP // 2):
            c0 = hh * GQA_GROUP * HEAD_DIM + pair * LANES
            oa = o[(2 * pair) * qb:(2 * pair + 1) * qb]
            ob = o[(2 * pair + 1) * qb:(2 * pair + 2) * qb]
            mix_s[pl.ds(r0, qb), c0:c0 + LANES] = jnp.where(lo_q, oa, ob).astype(BF16)

    unroll = LATENT_ATTN_UNROLL if latent else CONTEXT_ATTN_UNROLL

    def attend_blocks(it, carry):
        groups = [(it * unroll + u, hh) for u in range(unroll) for hh in range(N_KV_HEADS)]
        if latent:
            scores = scores_of(*groups[0])
            for g, group in enumerate(groups):
                cur = scores
                if g + 1 < len(groups):
                    scores = scores_of(*groups[g + 1])
                finish(*group, cur)
        else:
            for group in groups:
                finish(*group, scores_of(*group))
        return carry

    lax.fori_loop(0, TILE // (qb * unroll), attend_blocks, 0)

    def out_project(r0, n):
        return _mm(mix_s[r0:r0 + n, :], w_out_ref[...])

    bounds = _chunk_bounds(EPI_CHUNKS)
    mix_next = out_project(*bounds[0])
    for q, (r0, n) in enumerate(bounds):
        mix = mix_next
        if q + 1 < len(bounds):
            mix_next = out_project(*bounds[q + 1])
        y = DEEPNORM_ALPHA * x_ref[r0:r0 + n, :] + _mod_slice(mod_ref, row_m, 2) * mix
        y = _layer_norm(y, _layer_row(g_ref, layer), _layer_row(b_ref, layer))
        _store_permuted(x1_ref, y, r0, seq)


def _mixer_call(x, mods, mod_row, sink, w_in, w_pool, pool_scale, w_out, ln_g, ln_b, *,
                layer, latent, seq, n_tiles, in_tile0, cache_k=None, cache_v=None, rope=None,
                ffn_f32=(), next_mod_params=None):
    n_tok = n_tiles * TILE
    in_specs = [
        pl.BlockSpec(memory_space=pltpu.SMEM),
        pl.BlockSpec((TILE, D_MODEL), lambda i: (i + in_tile0, 0)),
        _whole_spec((MOD_ROWS, 6 * D_MODEL)),
        _layer_spec((D_MODEL, IN_W), layer),
        _layer_spec((N_POOL_GROUPS, POOL_GROUP_W, POOL_GROUP_W), layer),
        _whole_spec((DEPTH, POOL_W)),
        _layer_spec((MIX_W, D_MODEL), layer),
        _whole_spec((DEPTH, D_MODEL)),
        _whole_spec((DEPTH, D_MODEL)),
    ]
    args = [sink, x, mods, w_in, w_pool, pool_scale, w_out, ln_g, ln_b]
    pad = WINDOW if latent else 0
    scratch = [
        pltpu.VMEM((TILE, ATTN_W), BF16),
        pltpu.VMEM((N_KV_HEADS, TILE + 2 * pad, LANES), BF16),
        pltpu.VMEM((N_KV_HEADS, TILE + 2 * pad, LANES), BF16),
        pltpu.VMEM((TILE, MIX_W), BF16),
    ]
    x_out = jax.ShapeDtypeStruct((COL_BLOCKS, n_tok, LANES), F32)
    x_spec = _cols_spec()
    if latent:
        past = cache_k.shape[2]
        in_specs += [
            pl.BlockSpec((None, None, past, KV_W), lambda i: (i, layer, 0, 0)),
            pl.BlockSpec((None, None, past, KV_W), lambda i: (i, layer, 0, 0)),
            pl.BlockSpec((3, TILE, LANES), lambda i: (0, 0, 0), pipeline_mode=pl.Buffered(1)),
        ]
        args += [cache_k, cache_v, rope]
        scratch += [pltpu.VMEM((N_KV_HEADS, past, LANES), BF16),
                    pltpu.VMEM((N_KV_HEADS, past, LANES), BF16)]
        out_shape, out_specs = [x_out], [x_spec]
        for w in ffn_f32:
            rows = w.shape[1] // n_tiles
            assert rows * n_tiles == w.shape[1] and rows % 16 == 0
            in_specs.append(pl.BlockSpec((None, rows, w.shape[2]), lambda i: (layer, i, 0)))
            args.append(w)
            out_shape.append(jax.ShapeDtypeStruct(w.shape[1:], BF16))
            out_specs.append(pl.BlockSpec((rows, w.shape[2]), lambda i: (i, 0)))
    else:
        kv_out = jax.ShapeDtypeStruct((n_tok, KV_W), F32)
        kv_spec = pl.BlockSpec((TILE, KV_W), lambda i: (i, 0))
        out_shape = [x_out, kv_out, kv_out]
        out_specs = [x_spec, kv_spec, kv_spec]
        if next_mod_params is not None:
            mod_in_specs, mod_out_spec = _mod_specs(layer + 1, n_tiles)
            in_specs += mod_in_specs
            args += list(next_mod_params)
            out_shape.append(jax.ShapeDtypeStruct((MOD_ROWS, 6 * D_MODEL), F32))
            out_specs.append(mod_out_spec)
    return pl.pallas_call(
        functools.partial(_mixer_kernel, latent=latent, seq=seq, layer=layer,
                          next_mods=next_mod_params is not None, mod_row=mod_row),
        grid=(n_tiles,),
        in_specs=in_specs,
        out_specs=out_specs,
        out_shape=out_shape,
        scratch_shapes=scratch,
        compiler_params=pltpu.CompilerParams(
            dimension_semantics=("arbitrary",), vmem_limit_bytes=VMEM_LIMIT),
        name="mixer_latent" if latent else "mixer_context",
    )(*args)


def _ffn_kernel(*refs, groups, layer, mod_row):
    x_refs, rest = refs[:len(groups)], refs[len(groups):]
    step = pl.program_id(0)
    tile0 = 0
    for x_ref, (n_tiles, seq) in zip(x_refs, groups):
        tile_fn = functools.partial(_ffn_tile, x_ref, *rest, seq=seq, layer=layer,
                                    row_m=mod_row(step))
        if len(groups) == 1:
            tile_fn()
        else:
            pl.when((step >= tile0) & (step < tile0 + n_tiles))(tile_fn)
        tile0 += n_tiles


def _ffn_tile(x_ref, mod_ref, w_up_ref, cw_ref, cb_ref, w_down_ref, g_ref, b_ref, o_ref,
              h_s, u_s, act_s, stage_s, *, seq, layer, row_m):
    n_seq = TILE // seq
    sub = lax.broadcasted_iota(jnp.int32, (SUBLANES, FF_CHUNK), 0)

    h_s[...] = (_cols_load(x_ref, 0, TILE) * (1.0 + _mod_slice(mod_ref, row_m, 4))
                + _mod_slice(mod_ref, row_m, 3)).astype(BF16)

    def up_project(c):
        hb = h_s[...]
        for part in range(2):
            col = part * D_FF + c * FF_CHUNK
            u_s[c % 2, part] = jnp.dot(hb, w_up_ref[:, col:col + FF_CHUNK],
                                       preferred_element_type=F32)

    def conv_swiglu(c):
        for s in range(n_seq):
            halves = []
            for part in range(2):
                col = part * D_FF + c * FF_CHUNK
                cw = cw_ref[:, col:col + FF_CHUNK]
                cur = u_s[c % 2, part, s * seq:(s + 1) * seq, :]
                wrap_prev = jnp.where(sub == 0, 0.0, pltpu.roll(cur[seq - SUBLANES:], 1, axis=0))
                wrap_next = jnp.where(sub == SUBLANES - 1, 0.0,
                                      pltpu.roll(cur[:SUBLANES], SUBLANES - 1, axis=0))
                prev = jnp.concatenate([wrap_prev, cur[:seq - SUBLANES]], axis=0)
                nxt = jnp.concatenate([cur[SUBLANES:], wrap_next], axis=0)
                halves.append(prev * cw[0:1] + cur * cw[1:2] + nxt * cw[2:3]
                              + _layer_row(cb_ref, layer, col, FF_CHUNK))
            a, g = halves
            act_s[s * seq:(s + 1) * seq, c * FF_CHUNK:(c + 1) * FF_CHUNK] = (_silu(g) * a).astype(BF16)

    up_project(0)
    for c in range(N_FF_CHUNKS):
        if c + 1 < N_FF_CHUNKS:
            up_project(c + 1)
        conv_swiglu(c)

    def down_project(r0, n):
        return _mm(act_s[r0:r0 + n, :], w_down_ref[...])

    bounds = _chunk_bounds(EPI_CHUNKS)
    ff_next = down_project(*bounds[0])
    for q, (r0, n) in enumerate(bounds):
        ff = ff_next
        if q + 1 < len(bounds):
            ff_next = down_project(*bounds[q + 1])
        y = DEEPNORM_ALPHA * _cols_load(x_ref, r0, n) + _mod_slice(mod_ref, row_m, 5) * ff
        y = _layer_norm(y, _layer_row(g_ref, layer), _layer_row(b_ref, layer))
        _store_natural(o_ref, stage_s, y, r0, seq)


def _ffn_call(xs, groups, mods, mod_row, w_up, conv_w, conv_b, w_down, ln_g, ln_b, *, layer,
              name):
    n_steps = sum(n for n, _ in groups)
    scratch = [pltpu.VMEM((TILE, D_MODEL), BF16),
               pltpu.VMEM((2, 2, TILE, FF_CHUNK), F32),
               pltpu.VMEM((TILE, D_FF), BF16),
               pltpu.VMEM((COL_BLOCKS, EPI_ROWS, LANES), F32)]
    out_specs = pl.BlockSpec((TILE, D_MODEL), lambda i: (i, 0))
    out_shape = jax.ShapeDtypeStruct((n_steps * TILE, D_MODEL), F32)
    x_specs, tile0 = [], 0
    for n_tiles, _ in groups:
        x_specs.append(_cols_spec(
            lambda i, tile0=tile0, n_tiles=n_tiles: jnp.clip(i - tile0, 0, n_tiles - 1)))
        tile0 += n_tiles
    return pl.pallas_call(
        functools.partial(_ffn_kernel, groups=tuple(groups), layer=layer, mod_row=mod_row),
        grid=(n_steps,),
        in_specs=x_specs + [
            _whole_spec((MOD_ROWS, 6 * D_MODEL)),
            _whole_spec((D_MODEL, 2 * D_FF)),
            _layer_spec((3, 2 * D_FF), layer),
            _whole_spec((DEPTH, 2 * D_FF)),
            _whole_spec((D_FF, D_MODEL)),
            _whole_spec((DEPTH, D_MODEL)),
            _whole_spec((DEPTH, D_MODEL)),
        ],
        out_specs=out_specs,
        out_shape=out_shape,
        scratch_shapes=scratch,
        compiler_params=pltpu.CompilerParams(
            dimension_semantics=("arbitrary",), vmem_limit_bytes=VMEM_LIMIT),
        name=name,
    )(*xs, mods, w_up, conv_w, conv_b, w_down, ln_g, ln_b)


def _rope_tables(n_pos):
    half = HEAD_DIM // 2
    t = jnp.arange(n_pos)
    inv_freq = ROPE_BASE ** (-jnp.arange(0, half, 2, dtype=F32) / half)

    def ang(p):
        a = p.astype(F32)[:, None] * inv_freq[None, :]
        return jnp.concatenate([a, a], axis=-1)

    a = jnp.concatenate([ang(t // GRID_W), ang(t % GRID_W)], axis=-1)
    cos = jnp.cos(a)
    sin = jnp.sin(a)
    first = (jnp.arange(HEAD_DIM) % half) < half // 2
    sin_up = jnp.where(first[None, :], -sin, 0.0)
    sin_dn = jnp.where(first[None, :], 0.0, sin)
    tabs = jnp.stack([cos, sin_up, sin_dn])
    return jnp.concatenate([tabs, tabs], axis=-1).astype(F32)


def kernel(x_prompt, x_sample, cache_k, cache_v, c, c_ctx, w_mod, b_mod, w_in, attn_sink,
           w_pool, pool_scale, w_out, ln1_g, ln1_b, w_up, conv_w, conv_b, w_down, ln2_g, ln2_b):
    batch, seq, _ = x_prompt.shape
    dec_batch, dec_seq, _ = x_sample.shape
    past = cache_k.shape[2]
    assert TILE % seq == 0 and dec_seq == TILE and seq & (seq - 1) == 0
    assert (batch * seq) % TILE == 0 and past % 16 == 0

    cond = jnp.zeros((MOD_ROWS, D_MODEL), F32).at[0].set(c_ctx).at[1:1 + dec_batch].set(c)
    mod_params = (cond, w_mod, b_mod)
    mods = _modulation(*mod_params, layer=0)

    mixer_w = (attn_sink, w_in, w_pool, pool_scale, w_out, ln1_g, ln1_b)
    rope = _rope_tables(dec_seq)
    cache_k_r = cache_k.reshape(dec_batch, DEPTH, past, KV_W)
    cache_v_r = cache_v.reshape(dec_batch, DEPTH, past, KV_W)

    xc = x_prompt.reshape(batch * seq, D_MODEL)
    xl = x_sample.reshape(dec_batch * dec_seq, D_MODEL)
    ctx_tiles = batch * seq // TILE
    lat_tiles = dec_batch * dec_seq // TILE
    ctx_row = lambda i: 0
    lat_row = lambda i: i + 1
    ctx_group, lat_group = (ctx_tiles, seq), (lat_tiles, dec_seq)
    ks, vs = [], []
    for l in range(DEPTH):
        mods_l = mods
        ctx_out = _mixer_call(xc, mods_l, ctx_row, *mixer_w, layer=l, latent=False, seq=seq,
                              n_tiles=ctx_tiles, in_tile0=0,
                              next_mod_params=mod_params if l + 1 < DEPTH else None)
        xc, k_l, v_l = ctx_out[:3]
        mods = ctx_out[3] if l + 1 < DEPTH else None
        xl, w_up_b, w_down_b = _mixer_call(
            xl, mods_l, lat_row, *mixer_w, layer=l, latent=True, seq=dec_seq, n_tiles=lat_tiles,
            in_tile0=0, cache_k=cache_k_r, cache_v=cache_v_r, rope=rope, ffn_f32=(w_up, w_down))
        ffn_w = (w_up_b, conv_w, conv_b, w_down_b, ln2_g, ln2_b)
        xc = _ffn_call([xc], [ctx_group], mods_l, ctx_row, *ffn_w, layer=l, name="ffn_context")
        xl = _ffn_call([xl], [lat_group], mods_l, lat_row, *ffn_w, layer=l, name="ffn_latent")
        ks.append(k_l.reshape(batch, seq, KV_W))
        vs.append(v_l.reshape(batch, seq, KV_W))
    y_prompt = xc.reshape(batch, seq, D_MODEL)
    y_sample = xl.reshape(dec_batch, dec_seq, D_MODEL)
    cache_shape = (batch, DEPTH, seq, N_KV_HEADS, HEAD_DIM)
    return (y_prompt, y_sample, jnp.stack(ks, axis=1).reshape(cache_shape),
            jnp.stack(vs, axis=1).reshape(cache_shape))
```

```python
import functools

import jax
import jax.numpy as jnp
from jax import lax
from jax.experimental import pallas as pl
from jax.experimental.pallas import tpu as pltpu

D_MODEL = 1024
DEPTH = 4
N_HEADS = 8
N_KV_HEADS = 2
HEAD_DIM = 64
GQA_GROUP = N_HEADS // N_KV_HEADS
ATTN_W = N_HEADS * HEAD_DIM
KV_W = N_KV_HEADS * HEAD_DIM
POOL_WINDOWS = (2, 4, 8, 16)
N_POOL_GROUPS = 4
POOL_W = D_MODEL // 2
POOL_GROUP_W = POOL_W // N_POOL_GROUPS
MIX_W = ATTN_W + POOL_W
IN_W = ATTN_W + 2 * KV_W + POOL_W
D_FF = 2816
GRID_W = 64
WINDOW = 128
ROPE_BASE = 10000.0
LN_EPS = 1e-5
DEEPNORM_ALPHA = (2 * DEPTH) ** 0.25
ATTN_SCALE = HEAD_DIM ** -0.5
NEG_INF = -1e30
LOG2E = 1.4426950408889634

LANES = 128
SUBLANES = 8
TILE = 1024
FF_CHUNK = 256
N_FF_CHUNKS = D_FF // FF_CHUNK
EPI_CHUNKS = (256, 256, 256, 256)
EPI_ROWS = max(EPI_CHUNKS)
LATENT_ATTN_UNROLL = 4
CONTEXT_ATTN_UNROLL = 2
MOD_COLS = 1536
MOD_ROWS = 16
VMEM_LIMIT = 60000 * 1024

F32 = jnp.float32
BF16 = jnp.bfloat16


def _silu(x):
    return x / (1.0 + jnp.exp(-x))


def _layer_norm(y, g, b):
    mu = jnp.mean(y, axis=-1, keepdims=True)
    yc = y - mu
    var = jnp.mean(yc * yc, axis=-1, keepdims=True)
    return yc * lax.rsqrt(var + LN_EPS) * g + b


def _chunk_bounds(sizes):
    assert sum(sizes) == TILE
    return [(sum(sizes[:q]), n) for q, n in enumerate(sizes)]


def _mod_slice(mod_ref, row, j):
    return mod_ref[pl.ds(row, 1), j * D_MODEL:(j + 1) * D_MODEL]


def _layer_row(ref, layer, c0=0, n=None):
    return ref[layer:layer + 1, c0:(ref.shape[1] if n is None else c0 + n)]


def _mm(a, w):
    return lax.dot_general(a, w, (((1,), (0,)), ((), ())), preferred_element_type=F32)


COL_BLOCKS = D_MODEL // LANES


def _cols_load(ref, r0, n):
    return jnp.concatenate([ref[j, r0:r0 + n, :] for j in range(COL_BLOCKS)], axis=1)


def _cols_store(ref, r0, y):
    for j in range(COL_BLOCKS):
        ref[j, r0:r0 + y.shape[0], :] = y[:, j * LANES:(j + 1) * LANES]


def _store_permuted(ref, y, row0, seq):
    per = seq // SUBLANES
    for q in range(y.shape[0] // per):
        t0 = row0 + q * per
        s, a = t0 // seq, (t0 % seq) // per
        for j in range(COL_BLOCKS):
            ref[j, pl.ds(s * seq + a, per, stride=SUBLANES), :] = (
                y[q * per:(q + 1) * per, j * LANES:(j + 1) * LANES])


def _store_natural(o_ref, stage_ref, y, row0, seq):
    n = y.shape[0]
    per = seq // SUBLANES
    _cols_store(stage_ref, 0, y)
    span = min(n, seq)
    for q0 in range(0, n, span):
        s, b0 = (row0 + q0) // seq, ((row0 + q0) % seq) // SUBLANES
        for a in range(SUBLANES):
            t0 = s * seq + a * per + b0
            for j in range(COL_BLOCKS):
                o_ref[t0:t0 + span // SUBLANES, j * LANES:(j + 1) * LANES] = (
                    stage_ref[j, pl.ds(q0 + a, span // SUBLANES, stride=SUBLANES), :])


def _cols_spec(tile_of=lambda i: i):
    return pl.BlockSpec((COL_BLOCKS, TILE, LANES), lambda i: (0, tile_of(i), 0))


def _whole_spec(shape):
    idx = (0,) * len(shape)
    return pl.BlockSpec(tuple(shape), lambda i: idx, pipeline_mode=pl.Buffered(1))


def _layer_spec(shape, layer):
    idx = (layer,) + (0,) * len(shape)
    return pl.BlockSpec((None,) + tuple(shape), lambda i: idx, pipeline_mode=pl.Buffered(1))


def _mod_kernel(cond_ref, w_ref, b_ref, o_ref, *, layer):
    o_ref[...] = (_mm(_silu(cond_ref[...]).astype(BF16), w_ref[...])
                  + _layer_row(b_ref, layer))


def _mod_specs(layer, n_steps):
    cols = (6 * D_MODEL) // n_steps
    assert cols * n_steps == 6 * D_MODEL and cols % LANES == 0
    in_specs = [pl.BlockSpec((MOD_ROWS, D_MODEL), lambda j: (0, 0)),
                pl.BlockSpec((None, D_MODEL, cols), lambda j: (layer, 0, j)),
                pl.BlockSpec((DEPTH, cols), lambda j: (0, j))]
    return in_specs, pl.BlockSpec((MOD_ROWS, cols), lambda j: (0, j))


def _modulation(cond, w_mod, b_mod, layer):
    n_steps = (6 * D_MODEL) // MOD_COLS
    in_specs, out_spec = _mod_specs(layer, n_steps)
    return pl.pallas_call(
        functools.partial(_mod_kernel, layer=layer),
        grid=(n_steps,),
        in_specs=in_specs,
        out_specs=out_spec,
        out_shape=jax.ShapeDtypeStruct((MOD_ROWS, 6 * D_MODEL), F32),
        compiler_params=pltpu.CompilerParams(
            dimension_semantics=("arbitrary",), vmem_limit_bytes=VMEM_LIMIT),
        name="modulation",
    )(cond, w_mod, b_mod)


def _dup_halves(x):
    lo = lax.broadcasted_iota(jnp.int32, x.shape, 1) < HEAD_DIM
    xr = pltpu.roll(x, HEAD_DIM, axis=1)
    return jnp.where(lo, x, xr), jnp.where(lo, xr, x)


def _attn_scores(lhs, key_sets):
    scores = []
    for k2, _, bias in key_sets:
        s = lax.dot_general(lhs, k2, (((1,), (1,)), ((), ())), preferred_element_type=F32)
        if bias is not None:
            rows, cols = s.shape
            s = (s.reshape(GQA_GROUP, rows // GQA_GROUP, cols) + bias[None]).reshape(rows, cols)
        scores.append(s)
    return scores


def _softmax_pv(scores, sink_col, key_sets):
    def lane_blocks(a):
        return [a[:, j * LANES:(j + 1) * LANES] for j in range(a.shape[1] // LANES)]

    m_lanes = functools.reduce(jnp.maximum, [b for s in scores for b in lane_blocks(s)])
    m = jnp.maximum(jnp.max(m_lanes, axis=-1, keepdims=True), sink_col)
    e_lanes = None
    out = None
    for s, (_, v2, _) in zip(scores, key_sets):
        e = jnp.exp2(s - m)
        e_sum = functools.reduce(jnp.add, lane_blocks(e))
        e_lanes = e_sum if e_lanes is None else e_lanes + e_sum
        o = jnp.dot(e.astype(BF16), v2, preferred_element_type=F32)
        out = o if out is None else out + o
    denom = jnp.sum(e_lanes, axis=-1, keepdims=True) + jnp.exp2(sink_col - m)
    return out / denom


def _mixer_kernel(*refs, latent, seq, layer, next_mods, mod_row):
    if latent:
        (sink_ref, x_ref, mod_ref, w_in_ref, w_pool_ref, ps_ref, w_out_ref, g_ref, b_ref,
         kc_ref, vc_ref, rope_ref, wu_f32_ref, wd_f32_ref,
         x1_ref, wu_bf16_ref, wd_bf16_ref, q_s, k_s, v_s, mix_s, kc_s, vc_s) = refs
    elif next_mods:
        (sink_ref, x_ref, mod_ref, w_in_ref, w_pool_ref, ps_ref, w_out_ref, g_ref, b_ref,
         cond_ref, wm_ref, bm_ref,
         x1_ref, k_out_ref, v_out_ref, mods_next_ref, q_s, k_s, v_s, mix_s) = refs
    else:
        (sink_ref, x_ref, mod_ref, w_in_ref, w_pool_ref, ps_ref, w_out_ref, g_ref, b_ref,
         x1_ref, k_out_ref, v_out_ref, q_s, k_s, v_s, mix_s) = refs

    row_m = mod_row(pl.program_id(0))
    h = (x_ref[...] * (1.0 + _mod_slice(mod_ref, row_m, 1))
         + _mod_slice(mod_ref, row_m, 0)).astype(BF16)
    p = _mm(h, w_in_ref[:, ATTN_W + 2 * KV_W:])
    qkv = _mm(h, w_in_ref[:, :ATTN_W + 2 * KV_W])
    if latent:
        wu_bf16_ref[...] = wu_f32_ref[...].astype(BF16)
        wd_bf16_ref[...] = wd_f32_ref[...].astype(BF16)
    elif next_mods:
        _mod_kernel(cond_ref, wm_ref, bm_ref, mods_next_ref, layer=layer + 1)
    k = qkv[:, ATTN_W:ATTN_W + KV_W]
    v = qkv[:, ATTN_W + KV_W:ATTN_W + 2 * KV_W]

    n_seq = TILE // seq
    sub = lax.broadcasted_iota(jnp.int32, (SUBLANES, POOL_GROUP_W), 0)

    def edge_rows(a, first_fn, last_fn):
        pieces = []
        for s in range(n_seq):
            blk = a[s * seq:(s + 1) * seq]
            head = blk[:SUBLANES] if first_fn is None else first_fn(blk[:SUBLANES])
            tail = blk[seq - SUBLANES:] if last_fn is None else last_fn(blk[seq - SUBLANES:])
            pieces += [head, blk[SUBLANES:seq - SUBLANES], tail]
        return jnp.concatenate(pieces, axis=0)

    def shift(a, j):
        rolled = pltpu.roll(a, (-j) % TILE, axis=0)
        if j < 0:
            return edge_rows(rolled, lambda r: jnp.where(sub >= -j, r, 0.0), None)
        return edge_rows(rolled, None, lambda r: jnp.where(sub < SUBLANES - j, r, 0.0))

    def pool_group(g):
        pg = p[:, g * POOL_GROUP_W:(g + 1) * POOL_GROUP_W]
        half = POOL_WINDOWS[g] // 2
        left, right, n = shift(pg, -1), pg, 1
        while n < half:
            left = left + shift(left, -n)
            right = right + shift(right, n)
            n *= 2
        total = left + right
        first_cnt = jnp.minimum(sub + half, 2 * half).astype(F32)
        last_cnt = jnp.minimum(SUBLANES - sub + half, 2 * half).astype(F32)
        mean = edge_rows(total * (1.0 / (2 * half)),
                         lambda r: r * (2.0 * half) / first_cnt,
                         lambda r: r * (2.0 * half) / last_cnt)
        d = (mean - pg).astype(BF16)
        y = _mm(d, w_pool_ref[g])
        y = y * _layer_row(ps_ref, layer, g * POOL_GROUP_W, POOL_GROUP_W)
        mix_s[:, ATTN_W + g * POOL_GROUP_W:ATTN_W + (g + 1) * POOL_GROUP_W] = y.astype(BF16)

    if latent:
        cos = rope_ref[0]
        sin_up = rope_ref[1]
        sin_dn = rope_ref[2]

        def rope(t):
            return (t * cos + pltpu.roll(t, LANES - HEAD_DIM // 4, axis=1) * sin_up
                    + pltpu.roll(t, HEAD_DIM // 4, axis=1) * sin_dn)

        k = rope(k)
        pad = WINDOW
        zeros = jnp.zeros((pad, LANES), BF16)
        for hh in range(N_KV_HEADS):
            k_s[hh, 0:pad, :] = zeros
            k_s[hh, pad + TILE:2 * pad + TILE, :] = zeros
            v_s[hh, 0:pad, :] = zeros
            v_s[hh, pad + TILE:2 * pad + TILE, :] = zeros
        kc0, kc1 = _dup_halves(kc_ref[...])
        vc0, vc1 = _dup_halves(vc_ref[...])
        kc_s[0] = kc0.astype(BF16)
        kc_s[1] = kc1.astype(BF16)
        vc_s[0] = vc0.astype(BF16)
        vc_s[1] = vc1.astype(BF16)
    else:
        pad = 0
        k_out_ref[...] = k.reshape(TILE // seq, seq, KV_W)
        v_out_ref[...] = v.reshape(TILE // seq, seq, KV_W)

    for j in range(ATTN_W // LANES):
        pool_group(j)
        qj = qkv[:, j * LANES:(j + 1) * LANES]
        if latent:
            qj = rope(qj)
        q_s[:, j * LANES:(j + 1) * LANES] = (qj * (ATTN_SCALE * LOG2E)).astype(BF16)
    k0, k1 = _dup_halves(k)
    v0, v1 = _dup_halves(v)
    k_s[0, pad:pad + TILE, :] = k0.astype(BF16)
    k_s[1, pad:pad + TILE, :] = k1.astype(BF16)
    v_s[0, pad:pad + TILE, :] = v0.astype(BF16)
    v_s[1, pad:pad + TILE, :] = v1.astype(BF16)

    qb = WINDOW if latent else seq
    n_rows = GQA_GROUP * qb
    row = lax.broadcasted_iota(jnp.int32, (n_rows, 1), 0)
    lo_q = lax.broadcasted_iota(jnp.int32, (qb, LANES), 1) < HEAD_DIM
    if latent:
        r = lax.broadcasted_iota(jnp.int32, (qb, 3 * WINDOW), 0)
        c = lax.broadcasted_iota(jnp.int32, (qb, 3 * WINDOW), 1)
        in_band = jnp.abs(r - c + WINDOW) <= WINDOW

    def key_sets_of(i, hh):
        r0 = pl.multiple_of(i * qb, qb)
        if not latent:
            return [(k_s[hh, pl.ds(r0, qb), :], v_s[hh, pl.ds(r0, qb), :], None)]
        kpos = (i - 1) * WINDOW + c
        valid = in_band & (kpos >= 0) & (kpos < TILE)
        bias = jnp.where(valid, 0.0, NEG_INF * LOG2E).astype(F32)
        return [(kc_s[hh], vc_s[hh], None),
                (k_s[hh, pl.ds(r0, 3 * WINDOW), :], v_s[hh, pl.ds(r0, 3 * WINDOW), :], bias)]

    def scores_of(i, hh):
        r0 = pl.multiple_of(i * qb, qb)
        parts = []
        for pair in range(GQA_GROUP // 2):
            c0 = hh * GQA_GROUP * HEAD_DIM + pair * LANES
            qp = q_s[pl.ds(r0, qb), c0:c0 + LANES]
            parts.append(jnp.where(lo_q, qp, jnp.zeros_like(qp)))
            parts.append(jnp.where(lo_q, jnp.zeros_like(qp), qp))
        lhs = jnp.concatenate(parts, axis=0)
        return _attn_scores(lhs, key_sets_of(i, hh))

    def sink_column(hh):
        h0 = hh * GQA_GROUP
        col = jnp.full((n_rows, 1), sink_ref[layer, h0 + GQA_GROUP - 1] * LOG2E, F32)
        for g in range(GQA_GROUP - 2, -1, -1):
            col = jnp.where(row < (g + 1) * qb, sink_ref[layer, h0 + g] * LOG2E, col)
        return col

    sink_cols = [sink_column(hh) for hh in range(N_KV_HEADS)]

    def finish(i, hh, scores):
        r0 = pl.multiple_of(i * qb, qb)
        o = _softmax_pv(scores, sink_cols[hh], key_sets_of(i, hh))
        for pair in range(GQA_GROUP // 2):
            c0 = hh * GQA_GROUP * HEAD_DIM + pair * LANES
            oa = o[(2 * pair) * qb:(2 * pair + 1) * qb]
            ob = o[(2 * pair + 1) * qb:(2 * pair + 2) * qb]
            mix_s[pl.ds(r0, qb), c0:c0 + LANES] = jnp.where(lo_q, oa, ob).astype(BF16)

    unroll = LATENT_ATTN_UNROLL if latent else CONTEXT_ATTN_UNROLL

    def attend_blocks(it, carry):
        groups = [(it * unroll + u, hh) for u in range(unroll) for hh in range(N_KV_HEADS)]
        if latent:
            scores = scores_of(*groups[0])
            for g, group in enumerate(groups):
                cur = scores
                if g + 1 < len(groups):
                    scores = scores_of(*groups[g + 1])
                finish(*group, cur)
        else:
            for group in groups:
                finish(*group, scores_of(*group))
        return carry

    lax.fori_loop(0, TILE // (qb * unroll), attend_blocks, 0)

    def out_project(r0, n):
        return _mm(mix_s[r0:r0 + n, :], w_out_ref[...])

    bounds = _chunk_bounds(EPI_CHUNKS)
    mix_next = out_project(*bounds[0])
    for q, (r0, n) in enumerate(bounds):
        mix = mix_next
        if q + 1 < len(bounds):
            mix_next = out_project(*bounds[q + 1])
        y = DEEPNORM_ALPHA * x_ref[r0:r0 + n, :] + _mod_slice(mod_ref, row_m, 2) * mix
        y = _layer_norm(y, _layer_row(g_ref, layer), _layer_row(b_ref, layer))
        _store_permuted(x1_ref, y, r0, seq)


def _mixer_call(x, mods, mod_row, sink, w_in, w_pool, pool_scale, w_out, ln_g, ln_b, *,
                layer, latent, seq, n_tiles, in_tile0, cache_k=None, cache_v=None, rope=None,
                ffn_f32=(), next_mod_params=None):
    n_tok = n_tiles * TILE
    in_specs = [
        pl.BlockSpec(memory_space=pltpu.SMEM),
        pl.BlockSpec((TILE, D_MODEL), lambda i: (i + in_tile0, 0)),
        _whole_spec((MOD_ROWS, 6 * D_MODEL)),
        _layer_spec((D_MODEL, IN_W), layer),
        _layer_spec((N_POOL_GROUPS, POOL_GROUP_W, POOL_GROUP_W), layer),
        _whole_spec((DEPTH, POOL_W)),
        _layer_spec((MIX_W, D_MODEL), layer),
        _whole_spec((DEPTH, D_MODEL)),
        _whole_spec((DEPTH, D_MODEL)),
    ]
    args = [sink, x, mods, w_in, w_pool, pool_scale, w_out, ln_g, ln_b]
    pad = WINDOW if latent else 0
    scratch = [
        pltpu.VMEM((TILE, ATTN_W), BF16),
        pltpu.VMEM((N_KV_HEADS, TILE + 2 * pad, LANES), BF16),
        pltpu.VMEM((N_KV_HEADS, TILE + 2 * pad, LANES), BF16),
        pltpu.VMEM((TILE, MIX_W), BF16),
    ]
    x_out = jax.ShapeDtypeStruct((COL_BLOCKS, n_tok, LANES), F32)
    x_spec = _cols_spec()
    if latent:
        past = cache_k.shape[2]
        in_specs += [
            pl.BlockSpec((None, None, past, KV_W), lambda i: (i, layer, 0, 0)),
            pl.BlockSpec((None, None, past, KV_W), lambda i: (i, layer, 0, 0)),
            pl.BlockSpec((3, TILE, LANES), lambda i: (0, 0, 0), pipeline_mode=pl.Buffered(1)),
        ]
        args += [cache_k, cache_v, rope]
        scratch += [pltpu.VMEM((N_KV_HEADS, past, LANES), BF16),
                    pltpu.VMEM((N_KV_HEADS, past, LANES), BF16)]
        out_shape, out_specs = [x_out], [x_spec]
        for w in ffn_f32:
            rows = w.shape[1] // n_tiles
            assert rows * n_tiles == w.shape[1] and rows % 16 == 0
            in_specs.append(pl.BlockSpec((None, rows, w.shape[2]), lambda i: (layer, i, 0)))
            args.append(w)
            out_shape.append(jax.ShapeDtypeStruct(w.shape[1:], BF16))
            out_specs.append(pl.BlockSpec((rows, w.shape[2]), lambda i: (i, 0)))
    else:
        kv_out = jax.ShapeDtypeStruct((n_tok // seq, seq, KV_W), F32)
        kv_spec = pl.BlockSpec((TILE // seq, seq, KV_W), lambda i: (i, 0, 0))
        out_shape = [x_out, kv_out, kv_out]
        out_specs = [x_spec, kv_spec, kv_spec]
        if next_mod_params is not None:
            mod_in_specs, mod_out_spec = _mod_specs(layer + 1, n_tiles)
            in_specs += mod_in_specs
            args += list(next_mod_params)
            out_shape.append(jax.ShapeDtypeStruct((MOD_ROWS, 6 * D_MODEL), F32))
            out_specs.append(mod_out_spec)
    return pl.pallas_call(
        functools.partial(_mixer_kernel, latent=latent, seq=seq, layer=layer,
                          next_mods=next_mod_params is not None, mod_row=mod_row),
        grid=(n_tiles,),
        in_specs=in_specs,
        out_specs=out_specs,
        out_shape=out_shape,
        scratch_shapes=scratch,
        compiler_params=pltpu.CompilerParams(
            dimension_semantics=("arbitrary",), vmem_limit_bytes=VMEM_LIMIT),
        name="mixer_latent" if latent else "mixer_context",
    )(*args)


def _ffn_kernel(*refs, groups, layer, mod_row):
    x_refs, rest = refs[:len(groups)], refs[len(groups):]
    step = pl.program_id(0)
    tile0 = 0
    for x_ref, (n_tiles, seq) in zip(x_refs, groups):
        tile_fn = functools.partial(_ffn_tile, x_ref, *rest, seq=seq, layer=layer,
                                    row_m=mod_row(step))
        if len(groups) == 1:
            tile_fn()
        else:
            pl.when((step >= tile0) & (step < tile0 + n_tiles))(tile_fn)
        tile0 += n_tiles


def _ffn_tile(x_ref, mod_ref, w_up_ref, cw_ref, cb_ref, w_down_ref, g_ref, b_ref, o_ref,
              h_s, u_s, act_s, stage_s, *, seq, layer, row_m):
    n_seq = TILE // seq
    sub = lax.broadcasted_iota(jnp.int32, (SUBLANES, FF_CHUNK), 0)

    h_s[...] = (_cols_load(x_ref, 0, TILE) * (1.0 + _mod_slice(mod_ref, row_m, 4))
                + _mod_slice(mod_ref, row_m, 3)).astype(BF16)

    def up_project(c):
        hb = h_s[...]
        for part in range(2):
            col = part * D_FF + c * FF_CHUNK
            u_s[c % 2, part] = jnp.dot(hb, w_up_ref[:, col:col + FF_CHUNK],
                                       preferred_element_type=F32)

    def conv_swiglu(c):
        for s in range(n_seq):
            halves = []
            for part in range(2):
                col = part * D_FF + c * FF_CHUNK
                cw = cw_ref[:, col:col + FF_CHUNK]
                cur = u_s[c % 2, part, s * seq:(s + 1) * seq, :]
                wrap_prev = jnp.where(sub == 0, 0.0, pltpu.roll(cur[seq - SUBLANES:], 1, axis=0))
                wrap_next = jnp.where(sub == SUBLANES - 1, 0.0,
                                      pltpu.roll(cur[:SUBLANES], SUBLANES - 1, axis=0))
                prev = jnp.concatenate([wrap_prev, cur[:seq - SUBLANES]], axis=0)
                nxt = jnp.concatenate([cur[SUBLANES:], wrap_next], axis=0)
                halves.append(prev * cw[0:1] + cur * cw[1:2] + nxt * cw[2:3]
                              + _layer_row(cb_ref, layer, col, FF_CHUNK))
            a, g = halves
            act_s[s * seq:(s + 1) * seq, c * FF_CHUNK:(c + 1) * FF_CHUNK] = (_silu(g) * a).astype(BF16)

    up_project(0)
    for c in range(N_FF_CHUNKS):
        if c + 1 < N_FF_CHUNKS:
            up_project(c + 1)
        conv_swiglu(c)

    def down_project(r0, n):
        return _mm(act_s[r0:r0 + n, :], w_down_ref[...])

    bounds = _chunk_bounds(EPI_CHUNKS)
    ff_next = down_project(*bounds[0])
    for q, (r0, n) in enumerate(bounds):
        ff = ff_next
        if q + 1 < len(bounds):
            ff_next = down_project(*bounds[q + 1])
        y = DEEPNORM_ALPHA * _cols_load(x_ref, r0, n) + _mod_slice(mod_ref, row_m, 5) * ff
        y = _layer_norm(y, _layer_row(g_ref, layer), _layer_row(b_ref, layer))
        _store_natural(o_ref, stage_s, y, r0, seq)


def _ffn_call(xs, groups, mods, mod_row, w_up, conv_w, conv_b, w_down, ln_g, ln_b, *, layer,
              name):
    n_steps = sum(n for n, _ in groups)
    scratch = [pltpu.VMEM((TILE, D_MODEL), BF16),
               pltpu.VMEM((2, 2, TILE, FF_CHUNK), F32),
               pltpu.VMEM((TILE, D_FF), BF16),
               pltpu.VMEM((COL_BLOCKS, EPI_ROWS, LANES), F32)]
    out_specs = pl.BlockSpec((TILE, D_MODEL), lambda i: (i, 0))
    out_shape = jax.ShapeDtypeStruct((n_steps * TILE, D_MODEL), F32)
    x_specs, tile0 = [], 0
    for n_tiles, _ in groups:
        x_specs.append(_cols_spec(
            lambda i, tile0=tile0, n_tiles=n_tiles: jnp.clip(i - tile0, 0, n_tiles - 1)))
        tile0 += n_tiles
    return pl.pallas_call(
        functools.partial(_ffn_kernel, groups=tuple(groups), layer=layer, mod_row=mod_row),
        grid=(n_steps,),
        in_specs=x_specs + [
            _whole_spec((MOD_ROWS, 6 * D_MODEL)),
            _whole_spec((D_MODEL, 2 * D_FF)),
            _layer_spec((3, 2 * D_FF), layer),
            _whole_spec((DEPTH, 2 * D_FF)),
            _whole_spec((D_FF, D_MODEL)),
            _whole_spec((DEPTH, D_MODEL)),
            _whole_spec((DEPTH, D_MODEL)),
        ],
        out_specs=out_specs,
        out_shape=out_shape,
        scratch_shapes=scratch,
        compiler_params=pltpu.CompilerParams(
            dimension_semantics=("arbitrary",), vmem_limit_bytes=VMEM_LIMIT),
        name=name,
    )(*xs, mods, w_up, conv_w, conv_b, w_down, ln_g, ln_b)


def _rope_tables(n_pos):
    half = HEAD_DIM // 2
    t = jnp.arange(n_pos)
    inv_freq = ROPE_BASE ** (-jnp.arange(0, half, 2, dtype=F32) / half)

    def ang(p):
        a = p.astype(F32)[:, None] * inv_freq[None, :]
        return jnp.concatenate([a, a], axis=-1)

    a = jnp.concatenate([ang(t // GRID_W), ang(t % GRID_W)], axis=-1)
    cos = jnp.cos(a)
    sin = jnp.sin(a)
    first = (jnp.arange(HEAD_DIM) % half) < half // 2
    sin_up = jnp.where(first[None, :], -sin, 0.0)
    sin_dn = jnp.where(first[None, :], 0.0, sin)
    tabs = jnp.stack([cos, sin_up, sin_dn])
    return jnp.concatenate([tabs, tabs], axis=-1).astype(F32)


def kernel(x_prompt, x_sample, cache_k, cache_v, c, c_ctx, w_mod, b_mod, w_in, attn_sink,
           w_pool, pool_scale, w_out, ln1_g, ln1_b, w_up, conv_w, conv_b, w_down, ln2_g, ln2_b):
    batch, seq, _ = x_prompt.shape
    dec_batch, dec_seq, _ = x_sample.shape
    past = cache_k.shape[2]
    assert TILE % seq == 0 and dec_seq == TILE and seq & (seq - 1) == 0
    assert (batch * seq) % TILE == 0 and past % 16 == 0

    cond = jnp.zeros((MOD_ROWS, D_MODEL), F32).at[0].set(c_ctx).at[1:1 + dec_batch].set(c)
    mod_params = (cond, w_mod, b_mod)
    mods = _modulation(*mod_params, layer=0)

    mixer_w = (attn_sink, w_in, w_pool, pool_scale, w_out, ln1_g, ln1_b)
    rope = _rope_tables(dec_seq)
    cache_k_r = cache_k.reshape(dec_batch, DEPTH, past, KV_W)
    cache_v_r = cache_v.reshape(dec_batch, DEPTH, past, KV_W)

    xc = x_prompt.reshape(batch * seq, D_MODEL)
    xl = x_sample.reshape(dec_batch * dec_seq, D_MODEL)
    ctx_tiles = batch * seq // TILE
    lat_tiles = dec_batch * dec_seq // TILE
    ctx_row = lambda i: 0
    lat_row = lambda i: i + 1
    ctx_group, lat_group = (ctx_tiles, seq), (lat_tiles, dec_seq)
    ks, vs = [], []
    for l in range(DEPTH):
        mods_l = mods
        ctx_out = _mixer_call(xc, mods_l, ctx_row, *mixer_w, layer=l, latent=False, seq=seq,
                              n_tiles=ctx_tiles, in_tile0=0,
                              next_mod_params=mod_params if l + 1 < DEPTH else None)
        xc, k_l, v_l = ctx_out[:3]
        mods = ctx_out[3] if l + 1 < DEPTH else None
        xl, w_up_b, w_down_b = _mixer_call(
            xl, mods_l, lat_row, *mixer_w, layer=l, latent=True, seq=dec_seq, n_tiles=lat_tiles,
            in_tile0=0, cache_k=cache_k_r, cache_v=cache_v_r, rope=rope, ffn_f32=(w_up, w_down))
        ffn_w = (w_up_b, conv_w, conv_b, w_down_b, ln2_g, ln2_b)
        xc = _ffn_call([xc], [ctx_group], mods_l, ctx_row, *ffn_w, layer=l, name="ffn_context")
        xl = _ffn_call([xl], [lat_group], mods_l, lat_row, *ffn_w, layer=l, name="ffn_latent")
        ks.append(k_l)
        vs.append(v_l)
    y_prompt = xc.reshape(batch, seq, D_MODEL)
    y_sample = xl.reshape(dec_batch, dec_seq, D_MODEL)
    cache_shape = (batch, DEPTH, seq, N_KV_HEADS, HEAD_DIM)
    return (y_prompt, y_sample, jnp.stack(ks, axis=1).reshape(cache_shape),
            jnp.stack(vs, axis=1).reshape(cache_shape))
```

```python
import functools

import jax
import jax.numpy as jnp
from jax import lax
from jax.experimental import pallas as pl
from jax.experimental.pallas import tpu as pltpu

D_MODEL = 1024
DEPTH = 4
N_HEADS = 8
N_KV_HEADS = 2
HEAD_DIM = 64
GQA_GROUP = N_HEADS // N_KV_HEADS
ATTN_W = N_HEADS * HEAD_DIM
KV_W = N_KV_HEADS * HEAD_DIM
POOL_WINDOWS = (2, 4, 8, 16)
N_POOL_GROUPS = 4
POOL_W = D_MODEL // 2
POOL_GROUP_W = POOL_W // N_POOL_GROUPS
MIX_W = ATTN_W + POOL_W
IN_W = ATTN_W + 2 * KV_W + POOL_W
D_FF = 2816
GRID_W = 64
WINDOW = 128
ROPE_BASE = 10000.0
LN_EPS = 1e-5
DEEPNORM_ALPHA = (2 * DEPTH) ** 0.25
ATTN_SCALE = HEAD_DIM ** -0.5
NEG_INF = -1e30
LOG2E = 1.4426950408889634

LANES = 128
SUBLANES = 8
BF16_ROWS = 16
TILE = 1024
FF_CHUNK = 256
N_FF_CHUNKS = D_FF // FF_CHUNK
EPI_CHUNKS = (256, 256, 256, 256)
EPI_ROWS = max(EPI_CHUNKS)
LATENT_ATTN_UNROLL = 4
CONTEXT_ATTN_UNROLL = 2
MOD_COLS = 1536
MOD_ROWS = BF16_ROWS
VMEM_LIMIT = 60000 * 1024

F32 = jnp.float32
BF16 = jnp.bfloat16

assert max(POOL_WINDOWS) // 2 <= SUBLANES and ATTN_W // LANES == N_POOL_GROUPS


def _silu(x):
    return x / (1.0 + jnp.exp(-x))


def _layer_norm(y, g, b):
    mu = jnp.mean(y, axis=-1, keepdims=True)
    yc = y - mu
    var = jnp.mean(yc * yc, axis=-1, keepdims=True)
    return yc * lax.rsqrt(var + LN_EPS) * g + b


def _chunk_bounds(sizes):
    assert sum(sizes) == TILE
    return [(sum(sizes[:q]), n) for q, n in enumerate(sizes)]


def _mod_slice(mod_ref, row, j):
    return mod_ref[pl.ds(row, 1), j * D_MODEL:(j + 1) * D_MODEL]


def _layer_row(ref, layer, c0=0, n=None):
    return ref[layer:layer + 1, c0:(ref.shape[1] if n is None else c0 + n)]


def _mm(a, w):
    return lax.dot_general(a, w, (((1,), (0,)), ((), ())), preferred_element_type=F32)


COL_BLOCKS = D_MODEL // LANES


def _cols_load(ref, r0, n):
    return jnp.concatenate([ref[j, r0:r0 + n, :] for j in range(COL_BLOCKS)], axis=1)


def _cols_store(ref, r0, y):
    for j in range(COL_BLOCKS):
        ref[j, r0:r0 + y.shape[0], :] = y[:, j * LANES:(j + 1) * LANES]


def _store_permuted(ref, y, row0, seq):
    per = seq // SUBLANES
    for q in range(y.shape[0] // per):
        t0 = row0 + q * per
        s, a = t0 // seq, (t0 % seq) // per
        for j in range(COL_BLOCKS):
            ref[j, pl.ds(s * seq + a, per, stride=SUBLANES), :] = (
                y[q * per:(q + 1) * per, j * LANES:(j + 1) * LANES])


def _store_natural(o_ref, stage_ref, y, row0, seq):
    n = y.shape[0]
    per = seq // SUBLANES
    _cols_store(stage_ref, 0, y)
    span = min(n, seq)
    for q0 in range(0, n, span):
        s, b0 = (row0 + q0) // seq, ((row0 + q0) % seq) // SUBLANES
        for a in range(SUBLANES):
            t0 = s * seq + a * per + b0
            for j in range(COL_BLOCKS):
                o_ref[t0:t0 + span // SUBLANES, j * LANES:(j + 1) * LANES] = (
                    stage_ref[j, pl.ds(q0 + a, span // SUBLANES, stride=SUBLANES), :])


def _cols_spec():
    return pl.BlockSpec((COL_BLOCKS, TILE, LANES), lambda i: (0, i, 0))


def _whole_spec(shape):
    idx = (0,) * len(shape)
    return pl.BlockSpec(tuple(shape), lambda i: idx, pipeline_mode=pl.Buffered(1))


def _layer_spec(shape, layer):
    idx = (layer,) + (0,) * len(shape)
    return pl.BlockSpec((None,) + tuple(shape), lambda i: idx, pipeline_mode=pl.Buffered(1))


def _mod_kernel(cond_ref, w_ref, b_ref, o_ref, *, layer):
    o_ref[...] = (_mm(_silu(cond_ref[...]).astype(BF16), w_ref[...])
                  + _layer_row(b_ref, layer))


def _mod_specs(layer, n_steps):
    cols = (6 * D_MODEL) // n_steps
    assert cols * n_steps == 6 * D_MODEL and cols % LANES == 0
    in_specs = [pl.BlockSpec((MOD_ROWS, D_MODEL), lambda j: (0, 0)),
                pl.BlockSpec((None, D_MODEL, cols), lambda j: (layer, 0, j)),
                pl.BlockSpec((DEPTH, cols), lambda j: (0, j))]
    return in_specs, pl.BlockSpec((MOD_ROWS, cols), lambda j: (0, j))


def _modulation(cond, w_mod, b_mod, layer):
    n_steps = (6 * D_MODEL) // MOD_COLS
    in_specs, out_spec = _mod_specs(layer, n_steps)
    return pl.pallas_call(
        functools.partial(_mod_kernel, layer=layer),
        grid=(n_steps,),
        in_specs=in_specs,
        out_specs=out_spec,
        out_shape=jax.ShapeDtypeStruct((MOD_ROWS, 6 * D_MODEL), F32),
        compiler_params=pltpu.CompilerParams(
            dimension_semantics=("arbitrary",), vmem_limit_bytes=VMEM_LIMIT),
        name="modulation",
    )(cond, w_mod, b_mod)


def _dup_halves(x):
    lo = lax.broadcasted_iota(jnp.int32, x.shape, 1) < HEAD_DIM
    xr = pltpu.roll(x, HEAD_DIM, axis=1)
    return jnp.where(lo, x, xr), jnp.where(lo, xr, x)


def _attn_scores(lhs, key_sets):
    scores = []
    for k2, _, bias in key_sets:
        s = lax.dot_general(lhs, k2, (((1,), (1,)), ((), ())), preferred_element_type=F32)
        if bias is not None:
            rows, cols = s.shape
            s = (s.reshape(GQA_GROUP, rows // GQA_GROUP, cols) + bias[None]).reshape(rows, cols)
        scores.append(s)
    return scores


def _softmax_pv(scores, sink_col, key_sets):
    def lane_blocks(a):
        return [a[:, j * LANES:(j + 1) * LANES] for j in range(a.shape[1] // LANES)]

    m_lanes = functools.reduce(jnp.maximum, [b for s in scores for b in lane_blocks(s)])
    m = jnp.maximum(jnp.max(m_lanes, axis=-1, keepdims=True), sink_col)
    e_lanes = None
    out = None
    for s, (_, v2, _) in zip(scores, key_sets):
        e = jnp.exp2(s - m)
        e_sum = functools.reduce(jnp.add, lane_blocks(e))
        e_lanes = e_sum if e_lanes is None else e_lanes + e_sum
        o = jnp.dot(e.astype(BF16), v2, preferred_element_type=F32)
        out = o if out is None else out + o
    denom = jnp.sum(e_lanes, axis=-1, keepdims=True) + jnp.exp2(sink_col - m)
    return out / denom


def _mixer_kernel(*refs, latent, seq, layer, next_mods, mod_row):
    if latent:
        (sink_ref, x_ref, mod_ref, w_in_ref, w_pool_ref, ps_ref, w_out_ref, g_ref, b_ref,
         kc_ref, vc_ref, rope_ref, wu_f32_ref, wd_f32_ref,
         x1_ref, wu_bf16_ref, wd_bf16_ref, q_s, k_s, v_s, mix_s, kc_s, vc_s) = refs
    elif next_mods:
        (sink_ref, x_ref, mod_ref, w_in_ref, w_pool_ref, ps_ref, w_out_ref, g_ref, b_ref,
         cond_ref, wm_ref, bm_ref,
         x1_ref, k_out_ref, v_out_ref, mods_next_ref, q_s, k_s, v_s, mix_s) = refs
    else:
        (sink_ref, x_ref, mod_ref, w_in_ref, w_pool_ref, ps_ref, w_out_ref, g_ref, b_ref,
         x1_ref, k_out_ref, v_out_ref, q_s, k_s, v_s, mix_s) = refs

    row_m = mod_row(pl.program_id(0))
    h = (x_ref[...] * (1.0 + _mod_slice(mod_ref, row_m, 1))
         + _mod_slice(mod_ref, row_m, 0)).astype(BF16)
    p = _mm(h, w_in_ref[:, ATTN_W + 2 * KV_W:])
    qkv = _mm(h, w_in_ref[:, :ATTN_W + 2 * KV_W])
    if latent:
        wu_bf16_ref[...] = wu_f32_ref[...].astype(BF16)
        wd_bf16_ref[...] = wd_f32_ref[...].astype(BF16)
    elif next_mods:
        _mod_kernel(cond_ref, wm_ref, bm_ref, mods_next_ref, layer=layer + 1)
    k = qkv[:, ATTN_W:ATTN_W + KV_W]
    v = qkv[:, ATTN_W + KV_W:ATTN_W + 2 * KV_W]

    n_seq = TILE // seq
    sub = lax.broadcasted_iota(jnp.int32, (SUBLANES, POOL_GROUP_W), 0)

    def edge_rows(a, first_fn, last_fn):
        pieces = []
        for s in range(n_seq):
            blk = a[s * seq:(s + 1) * seq]
            head = blk[:SUBLANES] if first_fn is None else first_fn(blk[:SUBLANES])
            tail = blk[seq - SUBLANES:] if last_fn is None else last_fn(blk[seq - SUBLANES:])
            pieces += [head, blk[SUBLANES:seq - SUBLANES], tail]
        return jnp.concatenate(pieces, axis=0)

    def shift(a, j):
        rolled = pltpu.roll(a, (-j) % TILE, axis=0)
        if j < 0:
            return edge_rows(rolled, lambda r: jnp.where(sub >= -j, r, 0.0), None)
        return edge_rows(rolled, None, lambda r: jnp.where(sub < SUBLANES - j, r, 0.0))

    def pool_group(g):
        pg = p[:, g * POOL_GROUP_W:(g + 1) * POOL_GROUP_W]
        half = POOL_WINDOWS[g] // 2
        left, right, n = shift(pg, -1), pg, 1
        while n < half:
            left = left + shift(left, -n)
            right = right + shift(right, n)
            n *= 2
        total = left + right
        first_cnt = jnp.minimum(sub + half, 2 * half).astype(F32)
        last_cnt = jnp.minimum(SUBLANES - sub + half, 2 * half).astype(F32)
        mean = edge_rows(total * (1.0 / (2 * half)),
                         lambda r: r * (2.0 * half) / first_cnt,
                         lambda r: r * (2.0 * half) / last_cnt)
        d = (mean - pg).astype(BF16)
        y = _mm(d, w_pool_ref[g])
        y = y * _layer_row(ps_ref, layer, g * POOL_GROUP_W, POOL_GROUP_W)
        mix_s[:, ATTN_W + g * POOL_GROUP_W:ATTN_W + (g + 1) * POOL_GROUP_W] = y.astype(BF16)

    if latent:
        cos = rope_ref[0]
        sin_up = rope_ref[1]
        sin_dn = rope_ref[2]

        def rope(t):
            return (t * cos + pltpu.roll(t, LANES - HEAD_DIM // 4, axis=1) * sin_up
                    + pltpu.roll(t, HEAD_DIM // 4, axis=1) * sin_dn)

        k = rope(k)
        pad = WINDOW
        zeros = jnp.zeros((pad, LANES), BF16)
        for hh in range(N_KV_HEADS):
            k_s[hh, 0:pad, :] = zeros
            k_s[hh, pad + TILE:2 * pad + TILE, :] = zeros
            v_s[hh, 0:pad, :] = zeros
            v_s[hh, pad + TILE:2 * pad + TILE, :] = zeros
        kc0, kc1 = _dup_halves(kc_ref[...])
        vc0, vc1 = _dup_halves(vc_ref[...])
        kc_s[0] = kc0.astype(BF16)
        kc_s[1] = kc1.astype(BF16)
        vc_s[0] = vc0.astype(BF16)
        vc_s[1] = vc1.astype(BF16)
    else:
        pad = 0
        k_out_ref[...] = k.reshape(TILE // seq, seq, KV_W)
        v_out_ref[...] = v.reshape(TILE // seq, seq, KV_W)

    for j in range(ATTN_W // LANES):
        pool_group(j)
        qj = qkv[:, j * LANES:(j + 1) * LANES]
        if latent:
            qj = rope(qj)
        q_s[:, j * LANES:(j + 1) * LANES] = (qj * (ATTN_SCALE * LOG2E)).astype(BF16)
    k0, k1 = _dup_halves(k)
    v0, v1 = _dup_halves(v)
    k_s[0, pad:pad + TILE, :] = k0.astype(BF16)
    k_s[1, pad:pad + TILE, :] = k1.astype(BF16)
    v_s[0, pad:pad + TILE, :] = v0.astype(BF16)
    v_s[1, pad:pad + TILE, :] = v1.astype(BF16)

    qb = WINDOW if latent else seq
    n_rows = GQA_GROUP * qb
    row = lax.broadcasted_iota(jnp.int32, (n_rows, 1), 0)
    lo_q = lax.broadcasted_iota(jnp.int32, (qb, LANES), 1) < HEAD_DIM
    if latent:
        r = lax.broadcasted_iota(jnp.int32, (qb, 3 * WINDOW), 0)
        c = lax.broadcasted_iota(jnp.int32, (qb, 3 * WINDOW), 1)
        in_band = jnp.abs(r - c + WINDOW) <= WINDOW

    def key_sets_of(i, hh):
        r0 = pl.multiple_of(i * qb, qb)
        if not latent:
            return [(k_s[hh, pl.ds(r0, qb), :], v_s[hh, pl.ds(r0, qb), :], None)]
        kpos = (i - 1) * WINDOW + c
        valid = in_band & (kpos >= 0) & (kpos < TILE)
        bias = jnp.where(valid, 0.0, NEG_INF * LOG2E).astype(F32)
        return [(kc_s[hh], vc_s[hh], None),
                (k_s[hh, pl.ds(r0, 3 * WINDOW), :], v_s[hh, pl.ds(r0, 3 * WINDOW), :], bias)]

    def scores_of(i, hh):
        r0 = pl.multiple_of(i * qb, qb)
        parts = []
        for pair in range(GQA_GROUP // 2):
            c0 = hh * GQA_GROUP * HEAD_DIM + pair * LANES
            qp = q_s[pl.ds(r0, qb), c0:c0 + LANES]
            parts.append(jnp.where(lo_q, qp, jnp.zeros_like(qp)))
            parts.append(jnp.where(lo_q, jnp.zeros_like(qp), qp))
        lhs = jnp.concatenate(parts, axis=0)
        return _attn_scores(lhs, key_sets_of(i, hh))

    def sink_column(hh):
        h0 = hh * GQA_GROUP
        col = jnp.full((n_rows, 1), sink_ref[layer, h0 + GQA_GROUP - 1] * LOG2E, F32)
        for g in range(GQA_GROUP - 2, -1, -1):
            col = jnp.where(row < (g + 1) * qb, sink_ref[layer, h0 + g] * LOG2E, col)
        return col

    sink_cols = [sink_column(hh) for hh in range(N_KV_HEADS)]

    def finish(i, hh, scores):
        r0 = pl.multiple_of(i * qb, qb)
        o = _softmax_pv(scores, sink_cols[hh], key_sets_of(i, hh))
        for pair in range(GQA_GROUP // 2):
            c0 = hh * GQA_GROUP * HEAD_DIM + pair * LANES
            oa = o[(2 * pair) * qb:(2 * pair + 1) * qb]
            ob = o[(2 * pair + 1) * qb:(2 * pair + 2) * qb]
            mix_s[pl.ds(r0, qb), c0:c0 + LANES] = jnp.where(lo_q, oa, ob).astype(BF16)

    unroll = LATENT_ATTN_UNROLL if latent else CONTEXT_ATTN_UNROLL

    def attend_blocks(it, carry):
        groups = [(it * unroll + u, hh) for u in range(unroll) for hh in range(N_KV_HEADS)]
        if latent:
            scores = scores_of(*groups[0])
            for g, group in enumerate(groups):
                cur = scores
                if g + 1 < len(groups):
                    scores = scores_of(*groups[g + 1])
                finish(*group, cur)
        else:
            for group in groups:
                finish(*group, scores_of(*group))
        return carry

    lax.fori_loop(0, TILE // (qb * unroll), attend_blocks, 0)

    def out_project(r0, n):
        return _mm(mix_s[r0:r0 + n, :], w_out_ref[...])

    bounds = _chunk_bounds(EPI_CHUNKS)
    mix_next = out_project(*bounds[0])
    for q, (r0, n) in enumerate(bounds):
        mix = mix_next
        if q + 1 < len(bounds):
            mix_next = out_project(*bounds[q + 1])
        y = DEEPNORM_ALPHA * x_ref[r0:r0 + n, :] + _mod_slice(mod_ref, row_m, 2) * mix
        y = _layer_norm(y, _layer_row(g_ref, layer), _layer_row(b_ref, layer))
        _store_permuted(x1_ref, y, r0, seq)


def _mixer_call(x, mods, mod_row, sink, w_in, w_pool, pool_scale, w_out, ln_g, ln_b, *,
                layer, latent, seq, cache_k=None, cache_v=None, rope=None,
                ffn_f32=(), next_mod_params=None):
    n_tok = x.shape[0]
    n_tiles = n_tok // TILE
    in_specs = [
        pl.BlockSpec(memory_space=pltpu.SMEM),
        pl.BlockSpec((TILE, D_MODEL), lambda i: (i, 0)),
        _whole_spec((MOD_ROWS, 6 * D_MODEL)),
        _layer_spec((D_MODEL, IN_W), layer),
        _layer_spec((N_POOL_GROUPS, POOL_GROUP_W, POOL_GROUP_W), layer),
        _whole_spec((DEPTH, POOL_W)),
        _layer_spec((MIX_W, D_MODEL), layer),
        _whole_spec((DEPTH, D_MODEL)),
        _whole_spec((DEPTH, D_MODEL)),
    ]
    args = [sink, x, mods, w_in, w_pool, pool_scale, w_out, ln_g, ln_b]
    pad = WINDOW if latent else 0
    scratch = [
        pltpu.VMEM((TILE, ATTN_W), BF16),
        pltpu.VMEM((N_KV_HEADS, TILE + 2 * pad, LANES), BF16),
        pltpu.VMEM((N_KV_HEADS, TILE + 2 * pad, LANES), BF16),
        pltpu.VMEM((TILE, MIX_W), BF16),
    ]
    x_out = jax.ShapeDtypeStruct((COL_BLOCKS, n_tok, LANES), F32)
    x_spec = _cols_spec()
    if latent:
        past = cache_k.shape[2]
        in_specs += [
            pl.BlockSpec((None, None, past, KV_W), lambda i: (i, layer, 0, 0)),
            pl.BlockSpec((None, None, past, KV_W), lambda i: (i, layer, 0, 0)),
            pl.BlockSpec((3, TILE, LANES), lambda i: (0, 0, 0), pipeline_mode=pl.Buffered(1)),
        ]
        args += [cache_k, cache_v, rope]
        scratch += [pltpu.VMEM((N_KV_HEADS, past, LANES), BF16),
                    pltpu.VMEM((N_KV_HEADS, past, LANES), BF16)]
        out_shape, out_specs = [x_out], [x_spec]
        for w in ffn_f32:
            rows = w.shape[1] // n_tiles
            assert rows * n_tiles == w.shape[1] and rows % BF16_ROWS == 0
            in_specs.append(pl.BlockSpec((None, rows, w.shape[2]), lambda i: (layer, i, 0)))
            args.append(w)
            out_shape.append(jax.ShapeDtypeStruct(w.shape[1:], BF16))
            out_specs.append(pl.BlockSpec((rows, w.shape[2]), lambda i: (i, 0)))
    else:
        kv_out = jax.ShapeDtypeStruct((n_tok // seq, seq, KV_W), F32)
        kv_spec = pl.BlockSpec((TILE // seq, seq, KV_W), lambda i: (i, 0, 0))
        out_shape = [x_out, kv_out, kv_out]
        out_specs = [x_spec, kv_spec, kv_spec]
        if next_mod_params is not None:
            mod_in_specs, mod_out_spec = _mod_specs(layer + 1, n_tiles)
            in_specs += mod_in_specs
            args += list(next_mod_params)
            out_shape.append(jax.ShapeDtypeStruct((MOD_ROWS, 6 * D_MODEL), F32))
            out_specs.append(mod_out_spec)
    return pl.pallas_call(
        functools.partial(_mixer_kernel, latent=latent, seq=seq, layer=layer,
                          next_mods=next_mod_params is not None, mod_row=mod_row),
        grid=(n_tiles,),
        in_specs=in_specs,
        out_specs=out_specs,
        out_shape=out_shape,
        scratch_shapes=scratch,
        compiler_params=pltpu.CompilerParams(
            dimension_semantics=("arbitrary",), vmem_limit_bytes=VMEM_LIMIT),
        name="mixer_latent" if latent else "mixer_context",
    )(*args)


def _ffn_kernel(x_ref, mod_ref, w_up_ref, cw_ref, cb_ref, w_down_ref, g_ref, b_ref, o_ref,
                h_s, u_s, act_s, stage_s, *, seq, layer, mod_row):
    n_seq = TILE // seq
    row_m = mod_row(pl.program_id(0))
    sub = lax.broadcasted_iota(jnp.int32, (SUBLANES, FF_CHUNK), 0)

    h_s[...] = (_cols_load(x_ref, 0, TILE) * (1.0 + _mod_slice(mod_ref, row_m, 4))
                + _mod_slice(mod_ref, row_m, 3)).astype(BF16)

    def up_project(c):
        hb = h_s[...]
        for part in range(2):
            col = part * D_FF + c * FF_CHUNK
            u_s[c % 2, part] = jnp.dot(hb, w_up_ref[:, col:col + FF_CHUNK],
                                       preferred_element_type=F32)

    def conv_swiglu(c):
        for s in range(n_seq):
            halves = []
            for part in range(2):
                col = part * D_FF + c * FF_CHUNK
                cw = cw_ref[:, col:col + FF_CHUNK]
                cur = u_s[c % 2, part, s * seq:(s + 1) * seq, :]
                wrap_prev = jnp.where(sub == 0, 0.0, pltpu.roll(cur[seq - SUBLANES:], 1, axis=0))
                wrap_next = jnp.where(sub == SUBLANES - 1, 0.0,
                                      pltpu.roll(cur[:SUBLANES], SUBLANES - 1, axis=0))
                prev = jnp.concatenate([wrap_prev, cur[:seq - SUBLANES]], axis=0)
                nxt = jnp.concatenate([cur[SUBLANES:], wrap_next], axis=0)
                halves.append(prev * cw[0:1] + cur * cw[1:2] + nxt * cw[2:3]
                              + _layer_row(cb_ref, layer, col, FF_CHUNK))
            a, g = halves
            act_s[s * seq:(s + 1) * seq, c * FF_CHUNK:(c + 1) * FF_CHUNK] = (_silu(g) * a).astype(BF16)

    up_project(0)
    for c in range(N_FF_CHUNKS):
        if c + 1 < N_FF_CHUNKS:
            up_project(c + 1)
        conv_swiglu(c)

    def down_project(r0, n):
        return _mm(act_s[r0:r0 + n, :], w_down_ref[...])

    bounds = _chunk_bounds(EPI_CHUNKS)
    ff_next = down_project(*bounds[0])
    for q, (r0, n) in enumerate(bounds):
        ff = ff_next
        if q + 1 < len(bounds):
            ff_next = down_project(*bounds[q + 1])
        y = DEEPNORM_ALPHA * _cols_load(x_ref, r0, n) + _mod_slice(mod_ref, row_m, 5) * ff
        y = _layer_norm(y, _layer_row(g_ref, layer), _layer_row(b_ref, layer))
        _store_natural(o_ref, stage_s, y, r0, seq)


def _ffn_call(x, mods, mod_row, w_up, conv_w, conv_b, w_down, ln_g, ln_b, *, layer, seq, name):
    n_tok = x.shape[1]
    scratch = [pltpu.VMEM((TILE, D_MODEL), BF16),
               pltpu.VMEM((2, 2, TILE, FF_CHUNK), F32),
               pltpu.VMEM((TILE, D_FF), BF16),
               pltpu.VMEM((COL_BLOCKS, EPI_ROWS, LANES), F32)]
    out_specs = pl.BlockSpec((TILE, D_MODEL), lambda i: (i, 0))
    out_shape = jax.ShapeDtypeStruct((n_tok, D_MODEL), F32)
    return pl.pallas_call(
        functools.partial(_ffn_kernel, seq=seq, layer=layer, mod_row=mod_row),
        grid=(n_tok // TILE,),
        in_specs=[
            _cols_spec(),
            _whole_spec((MOD_ROWS, 6 * D_MODEL)),
            _whole_spec((D_MODEL, 2 * D_FF)),
            _layer_spec((3, 2 * D_FF), layer),
            _whole_spec((DEPTH, 2 * D_FF)),
            _whole_spec((D_FF, D_MODEL)),
            _whole_spec((DEPTH, D_MODEL)),
            _whole_spec((DEPTH, D_MODEL)),
        ],
        out_specs=out_specs,
        out_shape=out_shape,
        scratch_shapes=scratch,
        compiler_params=pltpu.CompilerParams(
            dimension_semantics=("arbitrary",), vmem_limit_bytes=VMEM_LIMIT),
        name=name,
    )(x, mods, w_up, conv_w, conv_b, w_down, ln_g, ln_b)


def _rope_tables(n_pos):
    half = HEAD_DIM // 2
    t = jnp.arange(n_pos)
    inv_freq = ROPE_BASE ** (-jnp.arange(0, half, 2, dtype=F32) / half)

    def ang(p):
        a = p.astype(F32)[:, None] * inv_freq[None, :]
        return jnp.concatenate([a, a], axis=-1)

    a = jnp.concatenate([ang(t // GRID_W), ang(t % GRID_W)], axis=-1)
    cos = jnp.cos(a)
    sin = jnp.sin(a)
    first = (jnp.arange(HEAD_DIM) % half) < half // 2
    sin_up = jnp.where(first[None, :], -sin, 0.0)
    sin_dn = jnp.where(first[None, :], 0.0, sin)
    tabs = jnp.stack([cos, sin_up, sin_dn])
    return jnp.concatenate([tabs, tabs], axis=-1).astype(F32)


def kernel(x_prompt, x_sample, cache_k, cache_v, c, c_ctx, w_mod, b_mod, w_in, attn_sink,
           w_pool, pool_scale, w_out, ln1_g, ln1_b, w_up, conv_w, conv_b, w_down, ln2_g, ln2_b):
    batch, seq, _ = x_prompt.shape
    dec_batch, dec_seq, _ = x_sample.shape
    past = cache_k.shape[2]
    assert TILE % seq == 0 and dec_seq == TILE and seq & (seq - 1) == 0
    assert (batch * seq) % TILE == 0 and past % BF16_ROWS == 0 and 1 + dec_batch <= MOD_ROWS

    cond = jnp.zeros((MOD_ROWS, D_MODEL), F32).at[0].set(c_ctx).at[1:1 + dec_batch].set(c)
    mod_params = (cond, w_mod, b_mod)
    mods = _modulation(*mod_params, layer=0)

    mixer_w = (attn_sink, w_in, w_pool, pool_scale, w_out, ln1_g, ln1_b)
    rope = _rope_tables(dec_seq)
    cache_k_r = cache_k.reshape(dec_batch, DEPTH, past, KV_W)
    cache_v_r = cache_v.reshape(dec_batch, DEPTH, past, KV_W)

    xc = x_prompt.reshape(batch * seq, D_MODEL)
    xl = x_sample.reshape(dec_batch * dec_seq, D_MODEL)
    ctx_row = lambda i: 0
    lat_row = lambda i: i + 1
    ks, vs = [], []
    for l in range(DEPTH):
        mods_l = mods
        ctx_out = _mixer_call(xc, mods_l, ctx_row, *mixer_w, layer=l, latent=False, seq=seq,
                              next_mod_params=mod_params if l + 1 < DEPTH else None)
        xc, k_l, v_l = ctx_out[:3]
        mods = ctx_out[3] if l + 1 < DEPTH else None
        xl, w_up_b, w_down_b = _mixer_call(
            xl, mods_l, lat_row, *mixer_w, layer=l, latent=True, seq=dec_seq,
            cache_k=cache_k_r, cache_v=cache_v_r, rope=rope, ffn_f32=(w_up, w_down))
        ffn_w = (w_up_b, conv_w, conv_b, w_down_b, ln2_g, ln2_b)
        xc = _ffn_call(xc, mods_l, ctx_row, *ffn_w, layer=l, seq=seq, name="ffn_context")
        xl = _ffn_call(xl, mods_l, lat_row, *ffn_w, layer=l, seq=dec_seq, name="ffn_latent")
        ks.append(k_l)
        vs.append(v_l)
    y_prompt = xc.reshape(batch, seq, D_MODEL)
    y_sample = xl.reshape(dec_batch, dec_seq, D_MODEL)
    cache_shape = (batch, DEPTH, seq, N_KV_HEADS, HEAD_DIM)
    return (y_prompt, y_sample, jnp.stack(ks, axis=1).reshape(cache_shape),
            jnp.stack(vs, axis=1).reshape(cache_shape))
```

```python
import functools

import jax
import jax.numpy as jnp
from jax import lax
from jax.experimental import pallas as pl
from jax.experimental.pallas import tpu as pltpu

D_MODEL = 1024
DEPTH = 4
N_HEADS = 8
N_KV_HEADS = 2
HEAD_DIM = 64
GQA_GROUP = N_HEADS // N_KV_HEADS
ATTN_W = N_HEADS * HEAD_DIM
KV_W = N_KV_HEADS * HEAD_DIM
POOL_WINDOWS = (2, 4, 8, 16)
N_POOL_GROUPS = 4
POOL_W = D_MODEL // 2
POOL_GROUP_W = POOL_W // N_POOL_GROUPS
MIX_W = ATTN_W + POOL_W
IN_W = ATTN_W + 2 * KV_W + POOL_W
D_FF = 2816
GRID_W = 64
WINDOW = 128
ROPE_BASE = 10000.0
LN_EPS = 1e-5
DEEPNORM_ALPHA = (2 * DEPTH) ** 0.25
ATTN_SCALE = HEAD_DIM ** -0.5
NEG_INF = -1e30
LOG2E = 1.4426950408889634

LANES = 128
SUBLANES = 8
BF16_ROWS = 16
TILE = 1024
SEG = 256
FF_CHUNK = 256
N_FF_CHUNKS = D_FF // FF_CHUNK
EPI_CHUNKS = (256, 256, 256, 256)
EPI_ROWS = max(EPI_CHUNKS)
LATENT_ATTN_UNROLL = 4
CONTEXT_ATTN_UNROLL = 2
MOD_COLS = 1536
MOD_ROWS = BF16_ROWS
VMEM_LIMIT = 60000 * 1024

F32 = jnp.float32
BF16 = jnp.bfloat16

assert max(POOL_WINDOWS) // 2 <= SUBLANES and ATTN_W // LANES == N_POOL_GROUPS


def _silu(x):
    return x / (1.0 + jnp.exp(-x))


def _layer_norm(y, g, b):
    mu = jnp.mean(y, axis=-1, keepdims=True)
    yc = y - mu
    var = jnp.mean(yc * yc, axis=-1, keepdims=True)
    return yc * lax.rsqrt(var + LN_EPS) * g + b


def _chunk_bounds(sizes):
    assert sum(sizes) == TILE
    return [(sum(sizes[:q]), n) for q, n in enumerate(sizes)]


def _mod_slice(mod_ref, row, j):
    return mod_ref[pl.ds(row, 1), j * D_MODEL:(j + 1) * D_MODEL]


def _layer_row(ref, layer, c0=0, n=None):
    return ref[layer:layer + 1, c0:(ref.shape[1] if n is None else c0 + n)]


def _mm(a, w):
    return lax.dot_general(a, w, (((1,), (0,)), ((), ())), preferred_element_type=F32)


COL_BLOCKS = D_MODEL // LANES


def _cols_load(ref, r0, n):
    return jnp.concatenate([ref[j, r0:r0 + n, :] for j in range(COL_BLOCKS)], axis=1)


def _cols_store(ref, r0, y):
    for j in range(COL_BLOCKS):
        ref[j, r0:r0 + y.shape[0], :] = y[:, j * LANES:(j + 1) * LANES]


def _store_permuted(ref, y, row0, seq):
    per = seq // SUBLANES
    for q in range(y.shape[0] // per):
        t0 = row0 + q * per
        s, a = t0 // seq, (t0 % seq) // per
        for j in range(COL_BLOCKS):
            ref[j, pl.ds(s * seq + a, per, stride=SUBLANES), :] = (
                y[q * per:(q + 1) * per, j * LANES:(j + 1) * LANES])


def _store_natural(o_ref, stage_ref, y, row0, seq):
    n = y.shape[0]
    per = seq // SUBLANES
    _cols_store(stage_ref, 0, y)
    span = min(n, seq)
    for q0 in range(0, n, span):
        s, b0 = (row0 + q0) // seq, ((row0 + q0) % seq) // SUBLANES
        for a in range(SUBLANES):
            t0 = s * seq + a * per + b0
            for j in range(COL_BLOCKS):
                o_ref[t0:t0 + span // SUBLANES, j * LANES:(j + 1) * LANES] = (
                    stage_ref[j, pl.ds(q0 + a, span // SUBLANES, stride=SUBLANES), :])


def _cols_spec():
    return pl.BlockSpec((COL_BLOCKS, TILE, LANES), lambda i: (0, i, 0))


def _whole_spec(shape):
    idx = (0,) * len(shape)
    return pl.BlockSpec(tuple(shape), lambda i: idx, pipeline_mode=pl.Buffered(1))


def _layer_spec(shape, layer):
    idx = (layer,) + (0,) * len(shape)
    return pl.BlockSpec((None,) + tuple(shape), lambda i: idx, pipeline_mode=pl.Buffered(1))


def _mod_kernel(cond_ref, w_ref, b_ref, o_ref, *, layer):
    o_ref[...] = (_mm(_silu(cond_ref[...]).astype(BF16), w_ref[...])
                  + _layer_row(b_ref, layer))


def _mod_specs(layer, n_steps):
    cols = (6 * D_MODEL) // n_steps
    assert cols * n_steps == 6 * D_MODEL and cols % LANES == 0
    in_specs = [pl.BlockSpec((MOD_ROWS, D_MODEL), lambda j: (0, 0)),
                pl.BlockSpec((None, D_MODEL, cols), lambda j: (layer, 0, j)),
                pl.BlockSpec((DEPTH, cols), lambda j: (0, j))]
    return in_specs, pl.BlockSpec((MOD_ROWS, cols), lambda j: (0, j))


def _modulation(cond, w_mod, b_mod, layer):
    n_steps = (6 * D_MODEL) // MOD_COLS
    in_specs, out_spec = _mod_specs(layer, n_steps)
    return pl.pallas_call(
        functools.partial(_mod_kernel, layer=layer),
        grid=(n_steps,),
        in_specs=in_specs,
        out_specs=out_spec,
        out_shape=jax.ShapeDtypeStruct((MOD_ROWS, 6 * D_MODEL), F32),
        compiler_params=pltpu.CompilerParams(
            dimension_semantics=("arbitrary",), vmem_limit_bytes=VMEM_LIMIT),
        name="modulation",
    )(cond, w_mod, b_mod)


def _dup_halves(x):
    lo = lax.broadcasted_iota(jnp.int32, x.shape, 1) < HEAD_DIM
    xr = pltpu.roll(x, HEAD_DIM, axis=1)
    return jnp.where(lo, x, xr), jnp.where(lo, xr, x)


def _attn_scores(lhs, key_sets):
    scores = []
    for k2, _, bias in key_sets:
        s = lax.dot_general(lhs, k2, (((1,), (1,)), ((), ())), preferred_element_type=F32)
        if bias is not None:
            rows, cols = s.shape
            s = (s.reshape(GQA_GROUP, rows // GQA_GROUP, cols) + bias[None]).reshape(rows, cols)
        scores.append(s)
    return scores


def _softmax_pv(scores, sink_col, key_sets):
    def lane_blocks(a):
        return [a[:, j * LANES:(j + 1) * LANES] for j in range(a.shape[1] // LANES)]

    m_lanes = functools.reduce(jnp.maximum, [b for s in scores for b in lane_blocks(s)])
    m = jnp.maximum(jnp.max(m_lanes, axis=-1, keepdims=True), sink_col)
    e_lanes = None
    out = None
    for s, (_, v2, _) in zip(scores, key_sets):
        e = jnp.exp2(s - m)
        e_sum = functools.reduce(jnp.add, lane_blocks(e))
        e_lanes = e_sum if e_lanes is None else e_lanes + e_sum
        o = jnp.dot(e.astype(BF16), v2, preferred_element_type=F32)
        out = o if out is None else out + o
    denom = jnp.sum(e_lanes, axis=-1, keepdims=True) + jnp.exp2(sink_col - m)
    return out / denom


def _mixer_kernel(*refs, latent, seq, layer, next_mods, mod_row):
    if latent:
        (sink_ref, x_ref, mod_ref, w_in_ref, w_pool_ref, ps_ref, w_out_ref, g_ref, b_ref,
         kc_ref, vc_ref, rope_ref, wu_f32_ref, wd_f32_ref,
         x1_ref, wu_bf16_ref, wd_bf16_ref, q_s, k_s, v_s, mix_s, kc_s, vc_s) = refs
    elif next_mods:
        (sink_ref, x_ref, mod_ref, w_in_ref, w_pool_ref, ps_ref, w_out_ref, g_ref, b_ref,
         cond_ref, wm_ref, bm_ref,
         x1_ref, k_out_ref, v_out_ref, mods_next_ref, q_s, k_s, v_s, mix_s) = refs
    else:
        (sink_ref, x_ref, mod_ref, w_in_ref, w_pool_ref, ps_ref, w_out_ref, g_ref, b_ref,
         x1_ref, k_out_ref, v_out_ref, q_s, k_s, v_s, mix_s) = refs

    row_m = mod_row(pl.program_id(0))
    h = (x_ref[...] * (1.0 + _mod_slice(mod_ref, row_m, 1))
         + _mod_slice(mod_ref, row_m, 0)).astype(BF16)
    p = _mm(h, w_in_ref[:, ATTN_W + 2 * KV_W:])
    qkv = _mm(h, w_in_ref[:, :ATTN_W + 2 * KV_W])
    if latent:
        wu_bf16_ref[...] = wu_f32_ref[...].astype(BF16)
        wd_bf16_ref[...] = wd_f32_ref[...].astype(BF16)
    elif next_mods:
        _mod_kernel(cond_ref, wm_ref, bm_ref, mods_next_ref, layer=layer + 1)
    k = qkv[:, ATTN_W:ATTN_W + KV_W]
    v = qkv[:, ATTN_W + KV_W:ATTN_W + 2 * KV_W]

    n_seq = TILE // seq
    sub = lax.broadcasted_iota(jnp.int32, (SUBLANES, POOL_GROUP_W), 0)

    def edge_rows(a, first_fn, last_fn):
        pieces = []
        for s in range(n_seq):
            blk = a[s * seq:(s + 1) * seq]
            head = blk[:SUBLANES] if first_fn is None else first_fn(blk[:SUBLANES])
            tail = blk[seq - SUBLANES:] if last_fn is None else last_fn(blk[seq - SUBLANES:])
            pieces += [head, blk[SUBLANES:seq - SUBLANES], tail]
        return jnp.concatenate(pieces, axis=0)

    def shift(a, j):
        rolled = pltpu.roll(a, (-j) % TILE, axis=0)
        if j < 0:
            return edge_rows(rolled, lambda r: jnp.where(sub >= -j, r, 0.0), None)
        return edge_rows(rolled, None, lambda r: jnp.where(sub < SUBLANES - j, r, 0.0))

    def pool_group(g):
        pg = p[:, g * POOL_GROUP_W:(g + 1) * POOL_GROUP_W]
        half = POOL_WINDOWS[g] // 2
        left, right, n = shift(pg, -1), pg, 1
        while n < half:
            left = left + shift(left, -n)
            right = right + shift(right, n)
            n *= 2
        total = left + right
        first_cnt = jnp.minimum(sub + half, 2 * half).astype(F32)
        last_cnt = jnp.minimum(SUBLANES - sub + half, 2 * half).astype(F32)
        mean = edge_rows(total * (1.0 / (2 * half)),
                         lambda r: r * (2.0 * half) / first_cnt,
                         lambda r: r * (2.0 * half) / last_cnt)
        d = (mean - pg).astype(BF16)
        y = _mm(d, w_pool_ref[g])
        y = y * _layer_row(ps_ref, layer, g * POOL_GROUP_W, POOL_GROUP_W)
        mix_s[:, ATTN_W + g * POOL_GROUP_W:ATTN_W + (g + 1) * POOL_GROUP_W] = y.astype(BF16)

    if latent:
        cos = rope_ref[0]
        sin_up = rope_ref[1]
        sin_dn = rope_ref[2]

        def rope(t):
            return (t * cos + pltpu.roll(t, LANES - HEAD_DIM // 4, axis=1) * sin_up
                    + pltpu.roll(t, HEAD_DIM // 4, axis=1) * sin_dn)

        k = rope(k)
        pad = WINDOW
        zeros = jnp.zeros((pad, LANES), BF16)
        for hh in range(N_KV_HEADS):
            k_s[hh, 0:pad, :] = zeros
            k_s[hh, pad + TILE:2 * pad + TILE, :] = zeros
            v_s[hh, 0:pad, :] = zeros
            v_s[hh, pad + TILE:2 * pad + TILE, :] = zeros
        kc0, kc1 = _dup_halves(kc_ref[...])
        vc0, vc1 = _dup_halves(vc_ref[...])
        kc_s[0] = kc0.astype(BF16)
        kc_s[1] = kc1.astype(BF16)
        vc_s[0] = vc0.astype(BF16)
        vc_s[1] = vc1.astype(BF16)
    else:
        pad = 0
        k_out_ref[...] = k.reshape(TILE // seq, seq, KV_W)
        v_out_ref[...] = v.reshape(TILE // seq, seq, KV_W)

    for j in range(ATTN_W // LANES):
        pool_group(j)
        qj = qkv[:, j * LANES:(j + 1) * LANES]
        if latent:
            qj = rope(qj)
        q_s[:, j * LANES:(j + 1) * LANES] = (qj * (ATTN_SCALE * LOG2E)).astype(BF16)
    k0, k1 = _dup_halves(k)
    v0, v1 = _dup_halves(v)
    k_s[0, pad:pad + TILE, :] = k0.astype(BF16)
    k_s[1, pad:pad + TILE, :] = k1.astype(BF16)
    v_s[0, pad:pad + TILE, :] = v0.astype(BF16)
    v_s[1, pad:pad + TILE, :] = v1.astype(BF16)

    qb = WINDOW if latent else seq
    n_rows = GQA_GROUP * qb
    row = lax.broadcasted_iota(jnp.int32, (n_rows, 1), 0)
    lo_q = lax.broadcasted_iota(jnp.int32, (qb, LANES), 1) < HEAD_DIM
    if latent:
        r = lax.broadcasted_iota(jnp.int32, (qb, 3 * WINDOW), 0)
        c = lax.broadcasted_iota(jnp.int32, (qb, 3 * WINDOW), 1)
        in_band = jnp.abs(r - c + WINDOW) <= WINDOW

    def key_sets_of(i, hh):
        r0 = pl.multiple_of(i * qb, qb)
        if not latent:
            return [(k_s[hh, pl.ds(r0, qb), :], v_s[hh, pl.ds(r0, qb), :], None)]
        kpos = (i - 1) * WINDOW + c
        valid = in_band & (kpos >= 0) & (kpos < TILE)
        bias = jnp.where(valid, 0.0, NEG_INF * LOG2E).astype(F32)
        return [(kc_s[hh], vc_s[hh], None),
                (k_s[hh, pl.ds(r0, 3 * WINDOW), :], v_s[hh, pl.ds(r0, 3 * WINDOW), :], bias)]

    def scores_of(i, hh):
        r0 = pl.multiple_of(i * qb, qb)
        parts = []
        for pair in range(GQA_GROUP // 2):
            c0 = hh * GQA_GROUP * HEAD_DIM + pair * LANES
            qp = q_s[pl.ds(r0, qb), c0:c0 + LANES]
            parts.append(jnp.where(lo_q, qp, jnp.zeros_like(qp)))
            parts.append(jnp.where(lo_q, jnp.zeros_like(qp), qp))
        lhs = jnp.concatenate(parts, axis=0)
        return _attn_scores(lhs, key_sets_of(i, hh))

    def sink_column(hh):
        h0 = hh * GQA_GROUP
        col = jnp.full((n_rows, 1), sink_ref[layer, h0 + GQA_GROUP - 1] * LOG2E, F32)
        for g in range(GQA_GROUP - 2, -1, -1):
            col = jnp.where(row < (g + 1) * qb, sink_ref[layer, h0 + g] * LOG2E, col)
        return col

    sink_cols = [sink_column(hh) for hh in range(N_KV_HEADS)]

    def finish(i, hh, scores):
        r0 = pl.multiple_of(i * qb, qb)
        o = _softmax_pv(scores, sink_cols[hh], key_sets_of(i, hh))
        for pair in range(GQA_GROUP // 2):
            c0 = hh * GQA_GROUP * HEAD_DIM + pair * LANES
            oa = o[(2 * pair) * qb:(2 * pair + 1) * qb]
            ob = o[(2 * pair + 1) * qb:(2 * pair + 2) * qb]
            mix_s[pl.ds(r0, qb), c0:c0 + LANES] = jnp.where(lo_q, oa, ob).astype(BF16)

    unroll = LATENT_ATTN_UNROLL if latent else CONTEXT_ATTN_UNROLL

    def attend_blocks(it, carry):
        groups = [(it * unroll + u, hh) for u in range(unroll) for hh in range(N_KV_HEADS)]
        if latent:
            scores = scores_of(*groups[0])
            for g, group in enumerate(groups):
                cur = scores
                if g + 1 < len(groups):
                    scores = scores_of(*groups[g + 1])
                finish(*group, cur)
        else:
            for group in groups:
                finish(*group, scores_of(*group))
        return carry

    lax.fori_loop(0, TILE // (qb * unroll), attend_blocks, 0)

    def out_project(r0, n):
        return _mm(mix_s[r0:r0 + n, :], w_out_ref[...])

    bounds = _chunk_bounds(EPI_CHUNKS)
    mix_next = out_project(*bounds[0])
    for q, (r0, n) in enumerate(bounds):
        mix = mix_next
        if q + 1 < len(bounds):
            mix_next = out_project(*bounds[q + 1])
        y = DEEPNORM_ALPHA * x_ref[r0:r0 + n, :] + _mod_slice(mod_ref, row_m, 2) * mix
        y = _layer_norm(y, _layer_row(g_ref, layer), _layer_row(b_ref, layer))
        _store_permuted(x1_ref, y, r0, SEG)


def _mixer_call(x, mods, mod_row, sink, w_in, w_pool, pool_scale, w_out, ln_g, ln_b, *,
                layer, latent, seq, n_tiles, tile0, cache_k=None, cache_v=None, rope=None,
                ffn_f32=(), next_mod_params=None):
    n_tok = n_tiles * TILE
    in_specs = [
        pl.BlockSpec(memory_space=pltpu.SMEM),
        pl.BlockSpec((TILE, D_MODEL), lambda i: (i + tile0, 0)),
        _whole_spec((MOD_ROWS, 6 * D_MODEL)),
        _layer_spec((D_MODEL, IN_W), layer),
        _layer_spec((N_POOL_GROUPS, POOL_GROUP_W, POOL_GROUP_W), layer),
        _whole_spec((DEPTH, POOL_W)),
        _layer_spec((MIX_W, D_MODEL), layer),
        _whole_spec((DEPTH, D_MODEL)),
        _whole_spec((DEPTH, D_MODEL)),
    ]
    args = [sink, x, mods, w_in, w_pool, pool_scale, w_out, ln_g, ln_b]
    pad = WINDOW if latent else 0
    scratch = [
        pltpu.VMEM((TILE, ATTN_W), BF16),
        pltpu.VMEM((N_KV_HEADS, TILE + 2 * pad, LANES), BF16),
        pltpu.VMEM((N_KV_HEADS, TILE + 2 * pad, LANES), BF16),
        pltpu.VMEM((TILE, MIX_W), BF16),
    ]
    x_out = jax.ShapeDtypeStruct((COL_BLOCKS, n_tok, LANES), F32)
    x_spec = _cols_spec()
    if latent:
        past = cache_k.shape[2]
        in_specs += [
            pl.BlockSpec((None, None, past, KV_W), lambda i: (i, layer, 0, 0)),
            pl.BlockSpec((None, None, past, KV_W), lambda i: (i, layer, 0, 0)),
            pl.BlockSpec((3, TILE, LANES), lambda i: (0, 0, 0), pipeline_mode=pl.Buffered(1)),
        ]
        args += [cache_k, cache_v, rope]
        scratch += [pltpu.VMEM((N_KV_HEADS, past, LANES), BF16),
                    pltpu.VMEM((N_KV_HEADS, past, LANES), BF16)]
        out_shape, out_specs = [x_out], [x_spec]
        for w in ffn_f32:
            rows = w.shape[1] // n_tiles
            assert rows * n_tiles == w.shape[1] and rows % BF16_ROWS == 0
            in_specs.append(pl.BlockSpec((None, rows, w.shape[2]), lambda i: (layer, i, 0)))
            args.append(w)
            out_shape.append(jax.ShapeDtypeStruct(w.shape[1:], BF16))
            out_specs.append(pl.BlockSpec((rows, w.shape[2]), lambda i: (i, 0)))
    else:
        kv_out = jax.ShapeDtypeStruct((n_tok // seq, seq, KV_W), F32)
        kv_spec = pl.BlockSpec((TILE // seq, seq, KV_W), lambda i: (i, 0, 0))
        out_shape = [x_out, kv_out, kv_out]
        out_specs = [x_spec, kv_spec, kv_spec]
        if next_mod_params is not None:
            mod_in_specs, mod_out_spec = _mod_specs(layer + 1, n_tiles)
            in_specs += mod_in_specs
            args += list(next_mod_params)
            out_shape.append(jax.ShapeDtypeStruct((MOD_ROWS, 6 * D_MODEL), F32))
            out_specs.append(mod_out_spec)
    return pl.pallas_call(
        functools.partial(_mixer_kernel, latent=latent, seq=seq, layer=layer,
                          next_mods=next_mod_params is not None, mod_row=mod_row),
        grid=(n_tiles,),
        in_specs=in_specs,
        out_specs=out_specs,
        out_shape=out_shape,
        scratch_shapes=scratch,
        compiler_params=pltpu.CompilerParams(
            dimension_semantics=("arbitrary",), vmem_limit_bytes=VMEM_LIMIT),
        name="mixer_latent" if latent else "mixer_context",
    )(*args)


def _ffn_kernel(xa_ref, xb_ref, mod_ref, w_up_ref, cw_ref, cb_ref, w_down_ref, g_ref, b_ref,
                o_ref, h_s, u_s, act_s, stage_s, *, a_tiles, a_seq, b_seq, layer, mod_row):
    assert a_seq == SEG and b_seq == TILE
    n_seg = TILE // SEG
    step = pl.program_id(0)
    seg_is_seq = step < a_tiles
    row_m = mod_row(step)
    sub = lax.broadcasted_iota(jnp.int32, (SUBLANES, FF_CHUNK), 0)

    def x_rows(r0, n):
        return jnp.where(seg_is_seq, _cols_load(xa_ref, r0, n), _cols_load(xb_ref, r0, n))

    h_s[...] = (x_rows(0, TILE) * (1.0 + _mod_slice(mod_ref, row_m, 4))
                + _mod_slice(mod_ref, row_m, 3)).astype(BF16)

    def up_project(c):
        hb = h_s[...]
        for part in range(2):
            col = part * D_FF + c * FF_CHUNK
            u_s[c % 2, part] = jnp.dot(hb, w_up_ref[:, col:col + FF_CHUNK],
                                       preferred_element_type=F32)

    def conv_swiglu(c):
        zero_row = jnp.zeros((SUBLANES, FF_CHUNK), F32)
        for s in range(n_seg):
            halves = []
            for part in range(2):
                col = part * D_FF + c * FF_CHUNK
                cw = cw_ref[:, col:col + FF_CHUNK]
                u_ref = u_s.at[c % 2, part]
                cur = u_ref[s * SEG:(s + 1) * SEG, :]
                before = (zero_row if s == 0 else
                          jnp.where(seg_is_seq, 0.0, u_ref[s * SEG - SUBLANES:s * SEG, :]))
                after = (zero_row if s == n_seg - 1 else
                         jnp.where(seg_is_seq, 0.0, u_ref[(s + 1) * SEG:(s + 1) * SEG + SUBLANES, :]))
                wrap_prev = pltpu.roll(jnp.where(sub == SUBLANES - 1, before, cur[SEG - SUBLANES:]),
                                       1, axis=0)
                wrap_next = pltpu.roll(jnp.where(sub == 0, after, cur[:SUBLANES]),
                                       SUBLANES - 1, axis=0)
                prev = jnp.concatenate([wrap_prev, cur[:SEG - SUBLANES]], axis=0)
                nxt = jnp.concatenate([cur[SUBLANES:], wrap_next], axis=0)
                halves.append(prev * cw[0:1] + cur * cw[1:2] + nxt * cw[2:3]
                              + _layer_row(cb_ref, layer, col, FF_CHUNK))
            a, g = halves
            act_s[s * SEG:(s + 1) * SEG, c * FF_CHUNK:(c + 1) * FF_CHUNK] = (_silu(g) * a).astype(BF16)

    up_project(0)
    for c in range(N_FF_CHUNKS):
        if c + 1 < N_FF_CHUNKS:
            up_project(c + 1)
        conv_swiglu(c)

    def down_project(r0, n):
        return _mm(act_s[r0:r0 + n, :], w_down_ref[...])

    bounds = _chunk_bounds(EPI_CHUNKS)
    ff_next = down_project(*bounds[0])
    for q, (r0, n) in enumerate(bounds):
        ff = ff_next
        if q + 1 < len(bounds):
            ff_next = down_project(*bounds[q + 1])
        y = DEEPNORM_ALPHA * x_rows(r0, n) + _mod_slice(mod_ref, row_m, 5) * ff
        y = _layer_norm(y, _layer_row(g_ref, layer), _layer_row(b_ref, layer))
        _store_natural(o_ref, stage_s, y, r0, SEG)


def _ffn_call(xa, xb, mods, mod_row, w_up, conv_w, conv_b, w_down, ln_g, ln_b, *, layer,
              a_tiles, b_tiles, a_seq, b_seq, name):
    n_steps = a_tiles + b_tiles
    scratch = [pltpu.VMEM((TILE, D_MODEL), BF16),
               pltpu.VMEM((2, 2, TILE, FF_CHUNK), F32),
               pltpu.VMEM((TILE, D_FF), BF16),
               pltpu.VMEM((COL_BLOCKS, EPI_ROWS, LANES), F32)]
    out_specs = pl.BlockSpec((TILE, D_MODEL), lambda i: (i, 0))
    out_shape = jax.ShapeDtypeStruct((n_steps * TILE, D_MODEL), F32)

    def held(tile0, n):
        return pl.BlockSpec((COL_BLOCKS, TILE, LANES),
                            lambda i: (0, jnp.clip(i - tile0, 0, max(n - 1, 0)), 0))

    return pl.pallas_call(
        functools.partial(_ffn_kernel, a_tiles=a_tiles, a_seq=a_seq, b_seq=b_seq, layer=layer,
                          mod_row=mod_row),
        grid=(n_steps,),
        in_specs=[
            held(0, a_tiles),
            held(a_tiles, b_tiles),
            _whole_spec((MOD_ROWS, 6 * D_MODEL)),
            _whole_spec((D_MODEL, 2 * D_FF)),
            _layer_spec((3, 2 * D_FF), layer),
            _whole_spec((DEPTH, 2 * D_FF)),
            _whole_spec((D_FF, D_MODEL)),
            _whole_spec((DEPTH, D_MODEL)),
            _whole_spec((DEPTH, D_MODEL)),
        ],
        out_specs=out_specs,
        out_shape=out_shape,
        scratch_shapes=scratch,
        compiler_params=pltpu.CompilerParams(
            dimension_semantics=("arbitrary",), vmem_limit_bytes=VMEM_LIMIT),
        name=name,
    )(xa, xb, mods, w_up, conv_w, conv_b, w_down, ln_g, ln_b)


def _rope_tables(n_pos):
    half = HEAD_DIM // 2
    t = jnp.arange(n_pos)
    inv_freq = ROPE_BASE ** (-jnp.arange(0, half, 2, dtype=F32) / half)

    def ang(p):
        a = p.astype(F32)[:, None] * inv_freq[None, :]
        return jnp.concatenate([a, a], axis=-1)

    a = jnp.concatenate([ang(t // GRID_W), ang(t % GRID_W)], axis=-1)
    cos = jnp.cos(a)
    sin = jnp.sin(a)
    first = (jnp.arange(HEAD_DIM) % half) < half // 2
    sin_up = jnp.where(first[None, :], -sin, 0.0)
    sin_dn = jnp.where(first[None, :], 0.0, sin)
    tabs = jnp.stack([cos, sin_up, sin_dn])
    return jnp.concatenate([tabs, tabs], axis=-1).astype(F32)


def kernel(x_prompt, x_sample, cache_k, cache_v, c, c_ctx, w_mod, b_mod, w_in, attn_sink,
           w_pool, pool_scale, w_out, ln1_g, ln1_b, w_up, conv_w, conv_b, w_down, ln2_g, ln2_b):
    batch, seq, _ = x_prompt.shape
    dec_batch, dec_seq, _ = x_sample.shape
    past = cache_k.shape[2]
    assert TILE % seq == 0 and dec_seq == TILE and seq & (seq - 1) == 0
    assert (batch * seq) % TILE == 0 and past % BF16_ROWS == 0 and 1 + dec_batch <= MOD_ROWS

    cond = jnp.zeros((MOD_ROWS, D_MODEL), F32).at[0].set(c_ctx).at[1:1 + dec_batch].set(c)
    mod_params = (cond, w_mod, b_mod)
    mods = _modulation(*mod_params, layer=0)

    mixer_w = (attn_sink, w_in, w_pool, pool_scale, w_out, ln1_g, ln1_b)
    rope = _rope_tables(dec_seq)
    cache_k_r = cache_k.reshape(dec_batch, DEPTH, past, KV_W)
    cache_v_r = cache_v.reshape(dec_batch, DEPTH, past, KV_W)

    xc = x_prompt.reshape(batch * seq, D_MODEL)
    xl = x_sample.reshape(dec_batch * dec_seq, D_MODEL)
    ctx_tiles, lat_tiles = batch * seq // TILE, dec_batch * dec_seq // TILE
    ctx_row = lambda i: 0
    lat_row = lambda i: i + 1
    both_row = lambda i: jnp.where(i < ctx_tiles, 0, i - ctx_tiles + 1)
    groups = dict(a_seq=seq, b_seq=dec_seq)
    ctx_tile0 = lat_tile0 = 0
    ks, vs = [], []
    for l in range(DEPTH):
        mods_l = mods
        ctx_out = _mixer_call(xc, mods_l, ctx_row, *mixer_w, layer=l, latent=False, seq=seq,
                              n_tiles=ctx_tiles, tile0=ctx_tile0,
                              next_mod_params=mod_params if l + 1 < DEPTH else None)
        xc, k_l, v_l = ctx_out[:3]
        mods = ctx_out[3] if l + 1 < DEPTH else None
        xl, w_up_b, w_down_b = _mixer_call(
            xl, mods_l, lat_row, *mixer_w, layer=l, latent=True, seq=dec_seq, n_tiles=lat_tiles,
            tile0=lat_tile0, cache_k=cache_k_r, cache_v=cache_v_r, rope=rope,
            ffn_f32=(w_up, w_down))
        ffn_w = (w_up_b, conv_w, conv_b, w_down_b, ln2_g, ln2_b)
        if l + 1 < DEPTH:
            xc = xl = _ffn_call(xc, xl, mods_l, both_row, *ffn_w, layer=l, a_tiles=ctx_tiles,
                                b_tiles=lat_tiles, name="ffn_both", **groups)
            ctx_tile0, lat_tile0 = 0, ctx_tiles
        else:
            xc = _ffn_call(xc, xc, mods_l, ctx_row, *ffn_w, layer=l, a_tiles=ctx_tiles,
                           b_tiles=0, name="ffn_context", **groups)
            xl = _ffn_call(xl, xl, mods_l, lat_row, *ffn_w, layer=l, a_tiles=0,
                           b_tiles=lat_tiles, name="ffn_latent", **groups)
        ks.append(k_l)
        vs.append(v_l)
    y_prompt = xc.reshape(batch, seq, D_MODEL)
    y_sample = xl.reshape(dec_batch, dec_seq, D_MODEL)
    cache_shape = (batch, DEPTH, seq, N_KV_HEADS, HEAD_DIM)
    return (y_prompt, y_sample, jnp.stack(ks, axis=1).reshape(cache_shape),
            jnp.stack(vs, axis=1).reshape(cache_shape))
```

```python
import functools

import jax
import jax.numpy as jnp
from jax import lax
from jax.experimental import pallas as pl
from jax.experimental.pallas import tpu as pltpu

D_MODEL = 1024
DEPTH = 4
N_HEADS = 8
N_KV_HEADS = 2
HEAD_DIM = 64
GQA_GROUP = N_HEADS // N_KV_HEADS
ATTN_W = N_HEADS * HEAD_DIM
KV_W = N_KV_HEADS * HEAD_DIM
POOL_WINDOWS = (2, 4, 8, 16)
N_POOL_GROUPS = 4
POOL_W = D_MODEL // 2
POOL_GROUP_W = POOL_W // N_POOL_GROUPS
MIX_W = ATTN_W + POOL_W
IN_W = ATTN_W + 2 * KV_W + POOL_W
D_FF = 2816
GRID_W = 64
WINDOW = 128
ROPE_BASE = 10000.0
LN_EPS = 1e-5
DEEPNORM_ALPHA = (2 * DEPTH) ** 0.25
ATTN_SCALE = HEAD_DIM ** -0.5
NEG_INF = -1e30
LOG2E = 1.4426950408889634

LANES = 128
SUBLANES = 8
BF16_ROWS = 16
TILE = 1024
SEG = 256
FF_CHUNK = 256
N_FF_CHUNKS = D_FF // FF_CHUNK
EPI_CHUNKS = (256, 256, 256, 256)
EPI_ROWS = max(EPI_CHUNKS)
LATENT_ATTN_UNROLL = 4
CONTEXT_ATTN_UNROLL = 2
MOD_COLS = 1536
MOD_ROWS = BF16_ROWS
VMEM_LIMIT = 60000 * 1024

F32 = jnp.float32
BF16 = jnp.bfloat16

assert max(POOL_WINDOWS) // 2 <= SUBLANES and ATTN_W // LANES == N_POOL_GROUPS


def _silu(x):
    return x / (1.0 + jnp.exp(-x))


def _layer_norm(y, g, b):
    mu = jnp.mean(y, axis=-1, keepdims=True)
    yc = y - mu
    var = jnp.mean(yc * yc, axis=-1, keepdims=True)
    return yc * lax.rsqrt(var + LN_EPS) * g + b


def _chunk_bounds(sizes):
    assert sum(sizes) == TILE
    return [(sum(sizes[:q]), n) for q, n in enumerate(sizes)]


def _mod_slice(mod_ref, row, j):
    return mod_ref[pl.ds(row, 1), j * D_MODEL:(j + 1) * D_MODEL]


def _layer_row(ref, layer, c0=0, n=None):
    return ref[layer:layer + 1, c0:(ref.shape[1] if n is None else c0 + n)]


def _mm(a, w):
    return lax.dot_general(a, w, (((1,), (0,)), ((), ())), preferred_element_type=F32)


COL_BLOCKS = D_MODEL // LANES


def _cols_load(ref, r0, n):
    return jnp.concatenate([ref[j, r0:r0 + n, :] for j in range(COL_BLOCKS)], axis=1)


def _cols_store(ref, r0, y):
    for j in range(COL_BLOCKS):
        ref[j, r0:r0 + y.shape[0], :] = y[:, j * LANES:(j + 1) * LANES]


def _store_permuted(ref, y, row0, seq):
    per = seq // SUBLANES
    for q in range(y.shape[0] // per):
        t0 = row0 + q * per
        s, a = t0 // seq, (t0 % seq) // per
        for j in range(COL_BLOCKS):
            ref[j, pl.ds(s * seq + a, per, stride=SUBLANES), :] = (
                y[q * per:(q + 1) * per, j * LANES:(j + 1) * LANES])


def _store_natural(o_ref, stage_ref, y, row0, seq):
    n = y.shape[0]
    per = seq // SUBLANES
    _cols_store(stage_ref, 0, y)
    span = min(n, seq)
    for q0 in range(0, n, span):
        s, b0 = (row0 + q0) // seq, ((row0 + q0) % seq) // SUBLANES
        for a in range(SUBLANES):
            t0 = s * seq + a * per + b0
            for j in range(COL_BLOCKS):
                o_ref[t0:t0 + span // SUBLANES, j * LANES:(j + 1) * LANES] = (
                    stage_ref[j, pl.ds(q0 + a, span // SUBLANES, stride=SUBLANES), :])


def _cols_spec():
    return pl.BlockSpec((COL_BLOCKS, TILE, LANES), lambda i: (0, i, 0))


def _whole_spec(shape):
    idx = (0,) * len(shape)
    return pl.BlockSpec(tuple(shape), lambda i: idx, pipeline_mode=pl.Buffered(1))


def _layer_spec(shape, layer):
    idx = (layer,) + (0,) * len(shape)
    return pl.BlockSpec((None,) + tuple(shape), lambda i: idx, pipeline_mode=pl.Buffered(1))


def _mod_kernel(cond_ref, w_ref, b_ref, o_ref, *, layer):
    o_ref[...] = (_mm(_silu(cond_ref[...]).astype(BF16), w_ref[...])
                  + _layer_row(b_ref, layer))


def _mod_specs(layer, n_steps):
    cols = (6 * D_MODEL) // n_steps
    assert cols * n_steps == 6 * D_MODEL and cols % LANES == 0
    in_specs = [pl.BlockSpec((MOD_ROWS, D_MODEL), lambda j: (0, 0)),
                pl.BlockSpec((None, D_MODEL, cols), lambda j: (layer, 0, j)),
                pl.BlockSpec((DEPTH, cols), lambda j: (0, j))]
    return in_specs, pl.BlockSpec((MOD_ROWS, cols), lambda j: (0, j))


def _modulation(cond, w_mod, b_mod, layer):
    n_steps = (6 * D_MODEL) // MOD_COLS
    in_specs, out_spec = _mod_specs(layer, n_steps)
    return pl.pallas_call(
        functools.partial(_mod_kernel, layer=layer),
        grid=(n_steps,),
        in_specs=in_specs,
        out_specs=out_spec,
        out_shape=jax.ShapeDtypeStruct((MOD_ROWS, 6 * D_MODEL), F32),
        compiler_params=pltpu.CompilerParams(
            dimension_semantics=("arbitrary",), vmem_limit_bytes=VMEM_LIMIT),
        name="modulation",
    )(cond, w_mod, b_mod)


def _dup_halves(x):
    lo = lax.broadcasted_iota(jnp.int32, x.shape, 1) < HEAD_DIM
    xr = pltpu.roll(x, HEAD_DIM, axis=1)
    return jnp.where(lo, x, xr), jnp.where(lo, xr, x)


def _attn_scores(lhs, key_sets):
    scores = []
    for k2, _, bias in key_sets:
        s = lax.dot_general(lhs, k2, (((1,), (1,)), ((), ())), preferred_element_type=F32)
        if bias is not None:
            rows, cols = s.shape
            s = (s.reshape(GQA_GROUP, rows // GQA_GROUP, cols) + bias[None]).reshape(rows, cols)
        scores.append(s)
    return scores


def _softmax_pv(scores, sink_col, key_sets):
    def lane_blocks(a):
        return [a[:, j * LANES:(j + 1) * LANES] for j in range(a.shape[1] // LANES)]

    m_lanes = functools.reduce(jnp.maximum, [b for s in scores for b in lane_blocks(s)])
    m = jnp.maximum(jnp.max(m_lanes, axis=-1, keepdims=True), sink_col)
    e_lanes = None
    out = None
    for s, (_, v2, _) in zip(scores, key_sets):
        e = jnp.exp2(s - m)
        e_sum = functools.reduce(jnp.add, lane_blocks(e))
        e_lanes = e_sum if e_lanes is None else e_lanes + e_sum
        o = jnp.dot(e.astype(BF16), v2, preferred_element_type=F32)
        out = o if out is None else out + o
    denom = jnp.sum(e_lanes, axis=-1, keepdims=True) + jnp.exp2(sink_col - m)
    return out / denom


def _mixer_kernel(*refs, latent, seq, layer, next_mods, mod_row):
    if latent:
        (sink_ref, x_ref, mod_ref, w_in_ref, w_pool_ref, ps_ref, w_out_ref, g_ref, b_ref,
         kc_ref, vc_ref, rope_ref, wu_f32_ref, wd_f32_ref,
         x1_ref, wu_bf16_ref, wd_bf16_ref, q_s, k_s, v_s, mix_s, kc_s, vc_s) = refs
    elif next_mods:
        (sink_ref, x_ref, mod_ref, w_in_ref, w_pool_ref, ps_ref, w_out_ref, g_ref, b_ref,
         cond_ref, wm_ref, bm_ref,
         x1_ref, k_out_ref, v_out_ref, mods_next_ref, q_s, k_s, v_s, mix_s) = refs
    else:
        (sink_ref, x_ref, mod_ref, w_in_ref, w_pool_ref, ps_ref, w_out_ref, g_ref, b_ref,
         x1_ref, k_out_ref, v_out_ref, q_s, k_s, v_s, mix_s) = refs

    row_m = mod_row(pl.program_id(0))
    h = (x_ref[...] * (1.0 + _mod_slice(mod_ref, row_m, 1))
         + _mod_slice(mod_ref, row_m, 0)).astype(BF16)
    p = _mm(h, w_in_ref[:, ATTN_W + 2 * KV_W:])
    qkv = _mm(h, w_in_ref[:, :ATTN_W + 2 * KV_W])
    if latent:
        wu_bf16_ref[...] = wu_f32_ref[...].astype(BF16)
        wd_bf16_ref[...] = wd_f32_ref[...].astype(BF16)
    elif next_mods:
        _mod_kernel(cond_ref, wm_ref, bm_ref, mods_next_ref, layer=layer + 1)
    k = qkv[:, ATTN_W:ATTN_W + KV_W]
    v = qkv[:, ATTN_W + KV_W:ATTN_W + 2 * KV_W]

    n_seq = TILE // seq
    sub = lax.broadcasted_iota(jnp.int32, (SUBLANES, POOL_GROUP_W), 0)

    def edge_rows(a, first_fn, last_fn):
        pieces = []
        for s in range(n_seq):
            blk = a[s * seq:(s + 1) * seq]
            head = blk[:SUBLANES] if first_fn is None else first_fn(blk[:SUBLANES])
            tail = blk[seq - SUBLANES:] if last_fn is None else last_fn(blk[seq - SUBLANES:])
            pieces += [head, blk[SUBLANES:seq - SUBLANES], tail]
        return jnp.concatenate(pieces, axis=0)

    def shift(a, j):
        rolled = pltpu.roll(a, (-j) % TILE, axis=0)
        if j < 0:
            return edge_rows(rolled, lambda r: jnp.where(sub >= -j, r, 0.0), None)
        return edge_rows(rolled, None, lambda r: jnp.where(sub < SUBLANES - j, r, 0.0))

    def pool_group(g):
        pg = p[:, g * POOL_GROUP_W:(g + 1) * POOL_GROUP_W]
        half = POOL_WINDOWS[g] // 2
        left, right, n = shift(pg, -1), pg, 1
        while n < half:
            left = left + shift(left, -n)
            right = right + shift(right, n)
            n *= 2
        total = left + right
        first_cnt = jnp.minimum(sub + half, 2 * half).astype(F32)
        last_cnt = jnp.minimum(SUBLANES - sub + half, 2 * half).astype(F32)
        mean = edge_rows(total * (1.0 / (2 * half)),
                         lambda r: r * (2.0 * half) / first_cnt,
                         lambda r: r * (2.0 * half) / last_cnt)
        d = (mean - pg).astype(BF16)
        y = _mm(d, w_pool_ref[g])
        y = y * _layer_row(ps_ref, layer, g * POOL_GROUP_W, POOL_GROUP_W)
        mix_s[:, ATTN_W + g * POOL_GROUP_W:ATTN_W + (g + 1) * POOL_GROUP_W] = y.astype(BF16)

    if latent:
        cos = rope_ref[0]
        sin_up = rope_ref[1]
        sin_dn = rope_ref[2]

        def rope(t):
            return (t * cos + pltpu.roll(t, LANES - HEAD_DIM // 4, axis=1) * sin_up
                    + pltpu.roll(t, HEAD_DIM // 4, axis=1) * sin_dn)

        k = rope(k)
        pad = WINDOW
        zeros = jnp.zeros((pad, LANES), BF16)
        for hh in range(N_KV_HEADS):
            k_s[hh, 0:pad, :] = zeros
            k_s[hh, pad + TILE:2 * pad + TILE, :] = zeros
            v_s[hh, 0:pad, :] = zeros
            v_s[hh, pad + TILE:2 * pad + TILE, :] = zeros
        kc0, kc1 = _dup_halves(kc_ref[...])
        vc0, vc1 = _dup_halves(vc_ref[...])
        kc_s[0] = kc0.astype(BF16)
        kc_s[1] = kc1.astype(BF16)
        vc_s[0] = vc0.astype(BF16)
        vc_s[1] = vc1.astype(BF16)
    else:
        pad = 0
        k_out_ref[...] = k.reshape(TILE // seq, seq, KV_W)
        v_out_ref[...] = v.reshape(TILE // seq, seq, KV_W)

    for j in range(ATTN_W // LANES):
        pool_group(j)
        qj = qkv[:, j * LANES:(j + 1) * LANES]
        if latent:
            qj = rope(qj)
        q_s[:, j * LANES:(j + 1) * LANES] = (qj * (ATTN_SCALE * LOG2E)).astype(BF16)
    k0, k1 = _dup_halves(k)
    v0, v1 = _dup_halves(v)
    k_s[0, pad:pad + TILE, :] = k0.astype(BF16)
    k_s[1, pad:pad + TILE, :] = k1.astype(BF16)
    v_s[0, pad:pad + TILE, :] = v0.astype(BF16)
    v_s[1, pad:pad + TILE, :] = v1.astype(BF16)

    qb = WINDOW if latent else seq
    n_rows = GQA_GROUP * qb
    row = lax.broadcasted_iota(jnp.int32, (n_rows, 1), 0)
    lo_q = lax.broadcasted_iota(jnp.int32, (qb, LANES), 1) < HEAD_DIM
    if latent:
        r = lax.broadcasted_iota(jnp.int32, (qb, 3 * WINDOW), 0)
        c = lax.broadcasted_iota(jnp.int32, (qb, 3 * WINDOW), 1)
        in_band = jnp.abs(r - c + WINDOW) <= WINDOW

    def key_sets_of(i, hh):
        r0 = pl.multiple_of(i * qb, qb)
        if not latent:
            return [(k_s[hh, pl.ds(r0, qb), :], v_s[hh, pl.ds(r0, qb), :], None)]
        kpos = (i - 1) * WINDOW + c
        valid = in_band & (kpos >= 0) & (kpos < TILE)
        bias = jnp.where(valid, 0.0, NEG_INF * LOG2E).astype(F32)
        return [(kc_s[hh], vc_s[hh], None),
                (k_s[hh, pl.ds(r0, 3 * WINDOW), :], v_s[hh, pl.ds(r0, 3 * WINDOW), :], bias)]

    def scores_of(i, hh):
        r0 = pl.multiple_of(i * qb, qb)
        parts = []
        for pair in range(GQA_GROUP // 2):
            c0 = hh * GQA_GROUP * HEAD_DIM + pair * LANES
            qp = q_s[pl.ds(r0, qb), c0:c0 + LANES]
            parts.append(jnp.where(lo_q, qp, jnp.zeros_like(qp)))
            parts.append(jnp.where(lo_q, jnp.zeros_like(qp), qp))
        lhs = jnp.concatenate(parts, axis=0)
        return _attn_scores(lhs, key_sets_of(i, hh))

    def sink_column(hh):
        h0 = hh * GQA_GROUP
        col = jnp.full((n_rows, 1), sink_ref[layer, h0 + GQA_GROUP - 1] * LOG2E, F32)
        for g in range(GQA_GROUP - 2, -1, -1):
            col = jnp.where(row < (g + 1) * qb, sink_ref[layer, h0 + g] * LOG2E, col)
        return col

    sink_cols = [sink_column(hh) for hh in range(N_KV_HEADS)]

    def finish(i, hh, scores):
        r0 = pl.multiple_of(i * qb, qb)
        o = _softmax_pv(scores, sink_cols[hh], key_sets_of(i, hh))
        for pair in range(GQA_GROUP // 2):
            c0 = hh * GQA_GROUP * HEAD_DIM + pair * LANES
            oa = o[(2 * pair) * qb:(2 * pair + 1) * qb]
            ob = o[(2 * pair + 1) * qb:(2 * pair + 2) * qb]
            mix_s[pl.ds(r0, qb), c0:c0 + LANES] = jnp.where(lo_q, oa, ob).astype(BF16)

    unroll = LATENT_ATTN_UNROLL if latent else CONTEXT_ATTN_UNROLL

    def attend_blocks(it, carry):
        groups = [(it * unroll + u, hh) for u in range(unroll) for hh in range(N_KV_HEADS)]
        if latent:
            scores = scores_of(*groups[0])
            for g, group in enumerate(groups):
                cur = scores
                if g + 1 < len(groups):
                    scores = scores_of(*groups[g + 1])
                finish(*group, cur)
        else:
            for group in groups:
                finish(*group, scores_of(*group))
        return carry

    lax.fori_loop(0, TILE // (qb * unroll), attend_blocks, 0)

    def out_project(r0, n):
        return _mm(mix_s[r0:r0 + n, :], w_out_ref[...])

    bounds = _chunk_bounds(EPI_CHUNKS)
    mix_next = out_project(*bounds[0])
    for q, (r0, n) in enumerate(bounds):
        mix = mix_next
        if q + 1 < len(bounds):
            mix_next = out_project(*bounds[q + 1])
        y = DEEPNORM_ALPHA * x_ref[r0:r0 + n, :] + _mod_slice(mod_ref, row_m, 2) * mix
        y = _layer_norm(y, _layer_row(g_ref, layer), _layer_row(b_ref, layer))
        _store_permuted(x1_ref, y, r0, SEG)


def _mixer_call(x, mods, mod_row, sink, w_in, w_pool, pool_scale, w_out, ln_g, ln_b, *,
                layer, latent, seq, n_tiles, tile0, cache_k=None, cache_v=None, rope=None,
                ffn_f32=(), next_mod_params=None):
    n_tok = n_tiles * TILE
    in_specs = [
        pl.BlockSpec(memory_space=pltpu.SMEM),
        pl.BlockSpec((TILE, D_MODEL), lambda i: (i + tile0, 0)),
        _whole_spec((MOD_ROWS, 6 * D_MODEL)),
        _layer_spec((D_MODEL, IN_W), layer),
        _layer_spec((N_POOL_GROUPS, POOL_GROUP_W, POOL_GROUP_W), layer),
        _whole_spec((DEPTH, POOL_W)),
        _layer_spec((MIX_W, D_MODEL), layer),
        _whole_spec((DEPTH, D_MODEL)),
        _whole_spec((DEPTH, D_MODEL)),
    ]
    args = [sink, x, mods, w_in, w_pool, pool_scale, w_out, ln_g, ln_b]
    pad = WINDOW if latent else 0
    scratch = [
        pltpu.VMEM((TILE, ATTN_W), BF16),
        pltpu.VMEM((N_KV_HEADS, TILE + 2 * pad, LANES), BF16),
        pltpu.VMEM((N_KV_HEADS, TILE + 2 * pad, LANES), BF16),
        pltpu.VMEM((TILE, MIX_W), BF16),
    ]
    x_out = jax.ShapeDtypeStruct((COL_BLOCKS, n_tok, LANES), F32)
    x_spec = _cols_spec()
    if latent:
        past = cache_k.shape[2]
        in_specs += [
            pl.BlockSpec((None, None, past, KV_W), lambda i: (i, layer, 0, 0)),
            pl.BlockSpec((None, None, past, KV_W), lambda i: (i, layer, 0, 0)),
            pl.BlockSpec((3, TILE, LANES), lambda i: (0, 0, 0), pipeline_mode=pl.Buffered(1)),
        ]
        args += [cache_k, cache_v, rope]
        scratch += [pltpu.VMEM((N_KV_HEADS, past, LANES), BF16),
                    pltpu.VMEM((N_KV_HEADS, past, LANES), BF16)]
        out_shape, out_specs = [x_out], [x_spec]
        for w in ffn_f32:
            rows = w.shape[1] // n_tiles
            assert rows * n_tiles == w.shape[1] and rows % BF16_ROWS == 0
            in_specs.append(pl.BlockSpec((None, rows, w.shape[2]), lambda i: (layer, i, 0)))
            args.append(w)
            out_shape.append(jax.ShapeDtypeStruct(w.shape[1:], BF16))
            out_specs.append(pl.BlockSpec((rows, w.shape[2]), lambda i: (i, 0)))
    else:
        kv_out = jax.ShapeDtypeStruct((n_tok // seq, seq, KV_W), F32)
        kv_spec = pl.BlockSpec((TILE // seq, seq, KV_W), lambda i: (i, 0, 0))
        out_shape = [x_out, kv_out, kv_out]
        out_specs = [x_spec, kv_spec, kv_spec]
        if next_mod_params is not None:
            mod_in_specs, mod_out_spec = _mod_specs(layer + 1, n_tiles)
            in_specs += mod_in_specs
            args += list(next_mod_params)
            out_shape.append(jax.ShapeDtypeStruct((MOD_ROWS, 6 * D_MODEL), F32))
            out_specs.append(mod_out_spec)
    return pl.pallas_call(
        functools.partial(_mixer_kernel, latent=latent, seq=seq, layer=layer,
                          next_mods=next_mod_params is not None, mod_row=mod_row),
        grid=(n_tiles,),
        in_specs=in_specs,
        out_specs=out_specs,
        out_shape=out_shape,
        scratch_shapes=scratch,
        compiler_params=pltpu.CompilerParams(
            dimension_semantics=("arbitrary",), vmem_limit_bytes=VMEM_LIMIT),
        name="mixer_latent" if latent else "mixer_context",
    )(*args)


def _ffn_kernel(xa_ref, xb_ref, mod_ref, w_up_ref, cw_ref, cb_ref, w_down_ref, g_ref, b_ref,
                o_ref, h_s, u_s, act_s, stage_s, *, a_tiles, b_tiles, a_seq, b_seq, layer,
                mod_row):
    assert a_seq == SEG and b_seq == TILE
    n_seg = TILE // SEG
    step = pl.program_id(0)
    seg_is_seq = True if b_tiles == 0 else False if a_tiles == 0 else step < a_tiles
    row_m = mod_row(step)
    sub = lax.broadcasted_iota(jnp.int32, (SUBLANES, FF_CHUNK), 0)

    def x_rows(r0, n):
        if b_tiles == 0:
            return _cols_load(xa_ref, r0, n)
        if a_tiles == 0:
            return _cols_load(xb_ref, r0, n)
        return jnp.where(seg_is_seq, _cols_load(xa_ref, r0, n), _cols_load(xb_ref, r0, n))

    h_s[...] = (x_rows(0, TILE) * (1.0 + _mod_slice(mod_ref, row_m, 4))
                + _mod_slice(mod_ref, row_m, 3)).astype(BF16)

    def up_project(c):
        hb = h_s[...]
        for part in range(2):
            col = part * D_FF + c * FF_CHUNK
            u_s[c % 2, part] = jnp.dot(hb, w_up_ref[:, col:col + FF_CHUNK],
                                       preferred_element_type=F32)

    def conv_swiglu(c):
        zero_row = jnp.zeros((SUBLANES, FF_CHUNK), F32)
        for s in range(n_seg):
            halves = []
            for part in range(2):
                col = part * D_FF + c * FF_CHUNK
                cw = cw_ref[:, col:col + FF_CHUNK]
                u_ref = u_s.at[c % 2, part]
                cur = u_ref[s * SEG:(s + 1) * SEG, :]
                before = (zero_row if s == 0 else
                          jnp.where(seg_is_seq, 0.0, u_ref[s * SEG - SUBLANES:s * SEG, :]))
                after = (zero_row if s == n_seg - 1 else
                         jnp.where(seg_is_seq, 0.0, u_ref[(s + 1) * SEG:(s + 1) * SEG + SUBLANES, :]))
                wrap_prev = pltpu.roll(jnp.where(sub == SUBLANES - 1, before, cur[SEG - SUBLANES:]),
                                       1, axis=0)
                wrap_next = pltpu.roll(jnp.where(sub == 0, after, cur[:SUBLANES]),
                                       SUBLANES - 1, axis=0)
                prev = jnp.concatenate([wrap_prev, cur[:SEG - SUBLANES]], axis=0)
                nxt = jnp.concatenate([cur[SUBLANES:], wrap_next], axis=0)
                halves.append(prev * cw[0:1] + cur * cw[1:2] + nxt * cw[2:3]
                              + _layer_row(cb_ref, layer, col, FF_CHUNK))
            a, g = halves
            act_s[s * SEG:(s + 1) * SEG, c * FF_CHUNK:(c + 1) * FF_CHUNK] = (_silu(g) * a).astype(BF16)

    up_project(0)
    for c in range(N_FF_CHUNKS):
        if c + 1 < N_FF_CHUNKS:
            up_project(c + 1)
        conv_swiglu(c)

    def down_project(r0, n):
        return _mm(act_s[r0:r0 + n, :], w_down_ref[...])

    bounds = _chunk_bounds(EPI_CHUNKS)
    ff_next = down_project(*bounds[0])
    for q, (r0, n) in enumerate(bounds):
        ff = ff_next
        if q + 1 < len(bounds):
            ff_next = down_project(*bounds[q + 1])
        y = DEEPNORM_ALPHA * x_rows(r0, n) + _mod_slice(mod_ref, row_m, 5) * ff
        y = _layer_norm(y, _layer_row(g_ref, layer), _layer_row(b_ref, layer))
        _store_natural(o_ref, stage_s, y, r0, SEG)


def _ffn_call(xa, xb, mods, mod_row, w_up, conv_w, conv_b, w_down, ln_g, ln_b, *, layer,
              a_tiles, b_tiles, a_seq, b_seq, name):
    n_steps = a_tiles + b_tiles
    scratch = [pltpu.VMEM((TILE, D_MODEL), BF16),
               pltpu.VMEM((2, 2, TILE, FF_CHUNK), F32),
               pltpu.VMEM((TILE, D_FF), BF16),
               pltpu.VMEM((COL_BLOCKS, EPI_ROWS, LANES), F32)]
    out_specs = pl.BlockSpec((TILE, D_MODEL), lambda i: (i, 0))
    out_shape = jax.ShapeDtypeStruct((n_steps * TILE, D_MODEL), F32)

    def held(tile0, n):
        return pl.BlockSpec((COL_BLOCKS, TILE, LANES),
                            lambda i: (0, jnp.clip(i - tile0, 0, max(n - 1, 0)), 0))

    return pl.pallas_call(
        functools.partial(_ffn_kernel, a_tiles=a_tiles, b_tiles=b_tiles, a_seq=a_seq,
                          b_seq=b_seq, layer=layer, mod_row=mod_row),
        grid=(n_steps,),
        in_specs=[
            held(0, a_tiles),
            held(a_tiles, b_tiles),
            _whole_spec((MOD_ROWS, 6 * D_MODEL)),
            _whole_spec((D_MODEL, 2 * D_FF)),
            _layer_spec((3, 2 * D_FF), layer),
            _whole_spec((DEPTH, 2 * D_FF)),
            _whole_spec((D_FF, D_MODEL)),
            _whole_spec((DEPTH, D_MODEL)),
            _whole_spec((DEPTH, D_MODEL)),
        ],
        out_specs=out_specs,
        out_shape=out_shape,
        scratch_shapes=scratch,
        compiler_params=pltpu.CompilerParams(
            dimension_semantics=("arbitrary",), vmem_limit_bytes=VMEM_LIMIT),
        name=name,
    )(xa, xb, mods, w_up, conv_w, conv_b, w_down, ln_g, ln_b)


def _rope_tables(n_pos):
    half = HEAD_DIM // 2
    t = jnp.arange(n_pos)
    inv_freq = ROPE_BASE ** (-jnp.arange(0, half, 2, dtype=F32) / half)

    def ang(p):
        a = p.astype(F32)[:, None] * inv_freq[None, :]
        return jnp.concatenate([a, a], axis=-1)

    a = jnp.concatenate([ang(t // GRID_W), ang(t % GRID_W)], axis=-1)
    cos = jnp.cos(a)
    sin = jnp.sin(a)
    first = (jnp.arange(HEAD_DIM) % half) < half // 2
    sin_up = jnp.where(first[None, :], -sin, 0.0)
    sin_dn = jnp.where(first[None, :], 0.0, sin)
    tabs = jnp.stack([cos, sin_up, sin_dn])
    return jnp.concatenate([tabs, tabs], axis=-1).astype(F32)


def kernel(x_prompt, x_sample, cache_k, cache_v, c, c_ctx, w_mod, b_mod, w_in, attn_sink,
           w_pool, pool_scale, w_out, ln1_g, ln1_b, w_up, conv_w, conv_b, w_down, ln2_g, ln2_b):
    batch, seq, _ = x_prompt.shape
    dec_batch, dec_seq, _ = x_sample.shape
    past = cache_k.shape[2]
    assert TILE % seq == 0 and dec_seq == TILE and seq & (seq - 1) == 0
    assert (batch * seq) % TILE == 0 and past % BF16_ROWS == 0 and 1 + dec_batch <= MOD_ROWS

    cond = jnp.zeros((MOD_ROWS, D_MODEL), F32).at[0].set(c_ctx).at[1:1 + dec_batch].set(c)
    mod_params = (cond, w_mod, b_mod)
    mods = _modulation(*mod_params, layer=0)

    mixer_w = (attn_sink, w_in, w_pool, pool_scale, w_out, ln1_g, ln1_b)
    rope = _rope_tables(dec_seq)
    cache_k_r = cache_k.reshape(dec_batch, DEPTH, past, KV_W)
    cache_v_r = cache_v.reshape(dec_batch, DEPTH, past, KV_W)

    xc = x_prompt.reshape(batch * seq, D_MODEL)
    xl = x_sample.reshape(dec_batch * dec_seq, D_MODEL)
    ctx_tiles, lat_tiles = batch * seq // TILE, dec_batch * dec_seq // TILE
    ctx_row = lambda i: 0
    lat_row = lambda i: i + 1
    both_row = lambda i: jnp.where(i < ctx_tiles, 0, i - ctx_tiles + 1)
    groups = dict(a_seq=seq, b_seq=dec_seq)
    ctx_tile0 = lat_tile0 = 0
    ks, vs = [], []
    for l in range(DEPTH):
        mods_l = mods
        ctx_out = _mixer_call(xc, mods_l, ctx_row, *mixer_w, layer=l, latent=False, seq=seq,
                              n_tiles=ctx_tiles, tile0=ctx_tile0,
                              next_mod_params=mod_params if l + 1 < DEPTH else None)
        xc, k_l, v_l = ctx_out[:3]
        mods = ctx_out[3] if l + 1 < DEPTH else None
        xl, w_up_b, w_down_b = _mixer_call(
            xl, mods_l, lat_row, *mixer_w, layer=l, latent=True, seq=dec_seq, n_tiles=lat_tiles,
            tile0=lat_tile0, cache_k=cache_k_r, cache_v=cache_v_r, rope=rope,
            ffn_f32=(w_up, w_down))
        ffn_w = (w_up_b, conv_w, conv_b, w_down_b, ln2_g, ln2_b)
        if l + 1 < DEPTH:
            xc = xl = _ffn_call(xc, xl, mods_l, both_row, *ffn_w, layer=l, a_tiles=ctx_tiles,
                                b_tiles=lat_tiles, name="ffn_both", **groups)
            ctx_tile0, lat_tile0 = 0, ctx_tiles
        else:
            xc = _ffn_call(xc, xc, mods_l, ctx_row, *ffn_w, layer=l, a_tiles=ctx_tiles,
                           b_tiles=0, name="ffn_context", **groups)
            xl = _ffn_call(xl, xl, mods_l, lat_row, *ffn_w, layer=l, a_tiles=0,
                           b_tiles=lat_tiles, name="ffn_latent", **groups)
        ks.append(k_l)
        vs.append(v_l)
    y_prompt = xc.reshape(batch, seq, D_MODEL)
    y_sample = xl.reshape(dec_batch, dec_seq, D_MODEL)
    cache_shape = (batch, DEPTH, seq, N_KV_HEADS, HEAD_DIM)
    return (y_prompt, y_sample, jnp.stack(ks, axis=1).reshape(cache_shape),
            jnp.stack(vs, axis=1).reshape(cache_shape))
```
